```python
import math
import jax, jax.numpy as jnp
from jax import lax
import numpy as np

D_MODEL = 4096
BATCH = 1
SEQ = 16384
DEPTH = 1
DEC_BATCH = 16
DEC_SEQ = 16
PAST_LEN = 1024

CHUNK = 64
EPS = 1e-6
D_CONV = D_MODEL // 2
CONV_WIDTH = 3
HEAD_DIM = 128
N_HEADS = (D_MODEL // 2) // HEAD_DIM
N_KV_HEADS = 4
GROUP = N_HEADS // N_KV_HEADS
D_ATTN = N_HEADS * HEAD_DIM
D_KV = N_KV_HEADS * HEAD_DIM
N_IDX_HEADS = 16
IDX_DIM = 64
IDX_SCALE = (N_IDX_HEADS ** -0.5) * (IDX_DIM ** -0.5)
TOPK_MAX = 256
Q_BLOCK = 128
N_BUCKETS = 32
MAX_DISTANCE = 128
N_EXPERTS = 32
EXPERT_TOP_K = 4
D_FF = D_MODEL
SWIGLU_LIMIT = 7.0
SWIGLU_ALPHA = 1.702
MOE_BLOCK = 128
IN_WIDTHS = (D_CONV, D_CONV, D_CONV, D_ATTN, D_KV, D_KV, N_IDX_HEADS * IDX_DIM, IDX_DIM, N_IDX_HEADS, D_MODEL, D_MODEL)
SPLIT_POINTS = tuple(int(p) for p in np.cumsum(IN_WIDTHS)[:-1])
D_IN = int(sum(IN_WIDTHS))

kernel_name = 'chunk_causal_conv_dsa_moe_step'


def rmsnorm(x, g):
    xf = x.astype(jnp.float32)
    xf = xf * lax.rsqrt(jnp.mean(xf * xf, axis=-1, keepdims=True) + EPS)
    return xf.astype(x.dtype) * g


def t5_bucket(rel):
    nb = N_BUCKETS // 2
    max_exact = nb // 2
    n = jnp.abs(rel)
    nf = jnp.maximum(n, 1).astype(jnp.float32)
    large = max_exact + (jnp.log(nf / max_exact) / math.log(MAX_DISTANCE / max_exact) * (nb - max_exact)).astype(jnp.int32)
    large = jnp.minimum(large, nb - 1)
    return jnp.where(rel > 0, nb, 0) + jnp.where(n < max_exact, n, large)


def short_conv(u, c_gate, b_gate, conv_w, conv_state):
    v = c_gate * u
    vp = jnp.concatenate([conv_state.astype(v.dtype), v], axis=1)
    t = v.shape[1]
    y = conv_w[0] * vp[:, 0:t]
    for j in range(1, CONV_WIDTH):
        y = y + conv_w[j] * vp[:, j:j + t]
    return b_gate * y, vp[:, vp.shape[1] - (CONV_WIDTH - 1):]


def gather_rows(a, idx):
    return jax.vmap(lambda ab, ib: ab[ib])(a, idx)


def dsa_attend(q, iq, iw, q_pos, k, v, ik, k_pos, rel_bias, k_top):
    b, tq = q.shape[0], q.shape[1]
    limit = (q_pos // CHUNK + 1) * CHUNK
    allowed = k_pos[None, :] < limit[:, None]
    idx_qk = jnp.einsum('bqhd,bsd->bqhs', iq, ik).astype(jnp.float32)
    score = jnp.einsum('bqh,bqhs->bqs', iw.astype(jnp.float32), jax.nn.relu(idx_qk))
    score = jnp.where(allowed[None], score, -jnp.inf)
    _, sel = lax.top_k(score, k_top)
    sel_pos = k_pos[sel]
    ok = sel_pos < limit[None, :, None]
    k_sel = gather_rows(k, sel)
    v_sel = gather_rows(v, sel)
    qg = q.reshape(b, tq, N_KV_HEADS, GROUP, HEAD_DIM)
    logits = jnp.einsum('bqgrd,bqjgd->bqgrj', qg, k_sel).astype(jnp.float32) * (HEAD_DIM ** -0.5)
    bias = rel_bias[t5_bucket(sel_pos - q_pos[None, :, None])].astype(jnp.float32)
    bias = jnp.moveaxis(bias, -1, 2).reshape(b, tq, N_KV_HEADS, GROUP, k_top)
    logits = jnp.where(ok[:, :, None, None, :], logits + bias, -jnp.inf)
    p = jax.nn.softmax(logits, axis=-1).astype(v.dtype)
    out = jnp.einsum('bqgrj,bqjgd->bqgrd', p, v_sel)
    return out.reshape(b, tq, D_ATTN)


def dsa_prompt(q, iq, iw, k, v, ik, rel_bias):
    b, t = q.shape[0], q.shape[1]
    n_blk = t // Q_BLOCK
    k_top = min(TOPK_MAX, t // 4)
    k_pos = jnp.arange(t, dtype=jnp.int32)

    def to_blocks(a):
        return jnp.moveaxis(a.reshape((b, n_blk, Q_BLOCK) + a.shape[2:]), 1, 0)

    def body(args):
        qb, iqb, iwb, pb = args
        return dsa_attend(qb, iqb, iwb, pb, k, v, ik, k_pos, rel_bias, k_top)

    out = lax.map(body, (to_blocks(q), to_blocks(iq), to_blocks(iw), k_pos.reshape(n_blk, Q_BLOCK)))
    return jnp.moveaxis(out, 0, 1).reshape(b, t, D_ATTN)


def dsa_sample(q, iq, iw, k, v, ik, past_k, past_v, past_ik, rel_bias):
    past, t = past_k.shape[1], q.shape[1]
    k_all = jnp.concatenate([past_k.astype(k.dtype), k], axis=1)
    v_all = jnp.concatenate([past_v.astype(v.dtype), v], axis=1)
    ik_all = jnp.concatenate([past_ik.astype(ik.dtype), ik], axis=1)
    k_pos = jnp.arange(past + t, dtype=jnp.int32)
    q_pos = past + jnp.arange(t, dtype=jnp.int32)
    k_top = min(TOPK_MAX, (past + t) // 4)
    return dsa_attend(q, iq, iw, q_pos, k_all, v_all, ik_all, k_pos, rel_bias, k_top)


def moe(h, router_w, router_b, w_up, b_up, w_down, b_down):
    t = h.shape[0]
    logits = (h @ router_w).astype(jnp.float32) + router_b.astype(jnp.float32)
    top_val, top_idx = lax.top_k(logits, EXPERT_TOP_K)
    gates = jax.nn.softmax(top_val, axis=-1)
    n = t * EXPERT_TOP_K
    e_flat = top_idx.reshape(n)
    tok_flat = jnp.arange(n, dtype=jnp.int32) // EXPERT_TOP_K
    g_flat = gates.reshape(n)
    order = jnp.argsort(e_flat)
    e_sorted = e_flat[order]
    counts = jnp.bincount(e_flat, length=N_EXPERTS)
    starts = jnp.cumsum(counts) - counts
    padded = (counts + MOE_BLOCK - 1) // MOE_BLOCK * MOE_BLOCK
    padded_end = jnp.cumsum(padded)
    padded_start = padded_end - padded
    dest = padded_start[e_sorted] + (jnp.arange(n, dtype=jnp.int32) - starts[e_sorted])
    n_blocks = -(-n // MOE_BLOCK) + N_EXPERTS
    n_rows = n_blocks * MOE_BLOCK
    buf_tok = jnp.zeros((n_rows,), jnp.int32).at[dest].set(tok_flat[order])
    buf_gate = jnp.zeros((n_rows,), jnp.float32).at[dest].set(g_flat[order])
    blk_start = jnp.arange(n_blocks, dtype=jnp.int32) * MOE_BLOCK
    blk_expert = jnp.minimum(jnp.searchsorted(padded_end, blk_start, side='right'), N_EXPERTS - 1)

    def expert_block(args):
        tok, gate, e = args
        hu = h[tok] @ w_up[e] + b_up[e]
        x_glu = jnp.minimum(hu[:, :D_FF], SWIGLU_LIMIT)
        x_lin = jnp.clip(hu[:, D_FF:], -SWIGLU_LIMIT, SWIGLU_LIMIT)
        act = x_glu * jax.nn.sigmoid(SWIGLU_ALPHA * x_glu) * (x_lin + 1.0)
        return (act @ w_down[e] + b_down[e]) * gate[:, None].astype(h.dtype)

    out = lax.map(expert_block, (buf_tok.reshape(n_blocks, MOE_BLOCK), buf_gate.reshape(n_blocks, MOE_BLOCK), blk_expert))
    return jax.ops.segment_sum(out.reshape(n_rows, h.shape[1]), buf_tok, num_segments=t)


def trunk_layer(x, conv_state, past_k, past_v, past_ik, rel_bias, g_mix, w_in, conv_w, w_branch_conv,
                w_branch_attn, w_out, g_ffn, router_w, router_b, w_up, b_up, w_down, b_down):
    b, t, _ = x.shape
    h = rmsnorm(x, g_mix)
    u, c_gate, b_gate, q, k, v, iq, ik, iw, gate_conv, gate_attn = jnp.split(h @ w_in, list(SPLIT_POINTS), axis=-1)
    if conv_state is None:
        conv_state = jnp.zeros((b, CONV_WIDTH - 1, D_CONV), x.dtype)
    y_conv, new_conv = short_conv(u, c_gate, b_gate, conv_w, conv_state)
    q = q.reshape(b, t, N_HEADS, HEAD_DIM)
    k = k.reshape(b, t, N_KV_HEADS, HEAD_DIM)
    v = v.reshape(b, t, N_KV_HEADS, HEAD_DIM)
    iq = iq.reshape(b, t, N_IDX_HEADS, IDX_DIM)
    iw = iw * IDX_SCALE
    if past_k is None:
        y_attn = dsa_prompt(q, iq, iw, k, v, ik, rel_bias)
    else:
        y_attn = dsa_sample(q, iq, iw, k, v, ik, past_k, past_v, past_ik, rel_bias)
    merged = jax.nn.sigmoid(gate_conv) * (y_conv @ w_branch_conv) + jax.nn.sigmoid(gate_attn) * (y_attn @ w_branch_attn)
    x = x + merged @ w_out
    h2 = rmsnorm(x, g_ffn)
    x = x + moe(h2.reshape(b * t, D_MODEL), router_w, router_b, w_up, b_up, w_down, b_down).reshape(b, t, D_MODEL)
    return x, k, v, ik, new_conv


def setup_inputs(seed: int = 0) -> dict:
    key = jax.random.key(seed)
    ks = jax.random.split(key, 24)
    f32 = jnp.float32

    def nrm(k_, shape, scale):
        return jax.random.normal(k_, shape, f32) * scale

    return {
        'x_prompt': nrm(ks[0], (BATCH, SEQ, D_MODEL), 1.0),
        'x_sample': nrm(ks[1], (DEC_BATCH, DEC_SEQ, D_MODEL), 1.0),
        'cache_k': nrm(ks[2], (DEPTH, DEC_BATCH, PAST_LEN, N_KV_HEADS, HEAD_DIM), 1.0),
        'cache_v': nrm(ks[3], (DEPTH, DEC_BATCH, PAST_LEN, N_KV_HEADS, HEAD_DIM), 1.0),
        'cache_idx_k': nrm(ks[4], (DEPTH, DEC_BATCH, PAST_LEN, IDX_DIM), 1.0),
        'state_conv': nrm(ks[5], (DEPTH, DEC_BATCH, CONV_WIDTH - 1, D_CONV), 1.0),
        'g_mix': 1.0 + nrm(ks[6], (DEPTH, D_MODEL), 0.02),
        'w_in': nrm(ks[7], (DEPTH, D_MODEL, D_IN), D_MODEL ** -0.5),
        'conv_w': nrm(ks[8], (DEPTH, CONV_WIDTH, D_CONV), CONV_WIDTH ** -0.5),
        'w_branch_conv': nrm(ks[9], (DEPTH, D_CONV, D_MODEL), D_CONV ** -0.5),
        'w_branch_attn': nrm(ks[10], (DEPTH, D_ATTN, D_MODEL), D_ATTN ** -0.5),
        'w_out': nrm(ks[11], (DEPTH, D_MODEL, D_MODEL), D_MODEL ** -0.5),
        'rel_bias': nrm(ks[12], (N_BUCKETS, N_HEADS), 0.5),
        'g_ffn': 1.0 + nrm(ks[13], (DEPTH, D_MODEL), 0.02),
        'router_w': nrm(ks[14], (DEPTH, D_MODEL, N_EXPERTS), D_MODEL ** -0.5),
        'router_b': nrm(ks[15], (DEPTH, N_EXPERTS), 0.01),
        'w_up': nrm(ks[16], (DEPTH, N_EXPERTS, D_MODEL, 2 * D_FF), D_MODEL ** -0.5),
        'b_up': nrm(ks[17], (DEPTH, N_EXPERTS, 2 * D_FF), 0.01),
        'w_down': nrm(ks[18], (DEPTH, N_EXPERTS, D_FF, D_MODEL), D_FF ** -0.5),
        'b_down': nrm(ks[19], (DEPTH, N_EXPERTS, D_MODEL), 0.01),
        'g_final': 1.0 + nrm(ks[20], (D_MODEL,), 0.02),
    }


def reference(x_prompt, x_sample, cache_k, cache_v, cache_idx_k, state_conv, g_mix, w_in, conv_w,
              w_branch_conv, w_branch_attn, w_out, rel_bias, g_ffn, router_w, router_b, w_up, b_up,
              w_down, b_down, g_final):
    xp, xs = x_prompt, x_sample
    kp, vp, ikp, cp, ksm, vsm, iks, csm = [], [], [], [], [], [], [], []
    for layer in range(DEPTH):
        lw = (g_mix[layer], w_in[layer], conv_w[layer], w_branch_conv[layer], w_branch_attn[layer], w_out[layer],
              g_ffn[layer], router_w[layer], router_b[layer], w_up[layer], b_up[layer], w_down[layer], b_down[layer])
        xp, k1, v1, ik1, c1 = trunk_layer(xp, None, None, None, None, rel_bias, *lw)
        xs, k2, v2, ik2, c2 = trunk_layer(xs, state_conv[layer], cache_k[layer], cache_v[layer],
                                          cache_idx_k[layer], rel_bias, *lw)
        kp.append(k1); vp.append(v1); ikp.append(ik1); cp.append(c1)
        ksm.append(k2); vsm.append(v2); iks.append(ik2); csm.append(c2)
    y_prompt = rmsnorm(xp, g_final)
    y_sample = rmsnorm(xs, g_final)
    return (y_prompt, y_sample, jnp.stack(kp), jnp.stack(vp), jnp.stack(ikp), jnp.stack(cp),
            jnp.stack(ksm), jnp.stack(vsm), jnp.stack(iks), jnp.stack(csm))
```

```python
import functools
import math

import numpy as np
import jax
import jax.numpy as jnp
from jax import lax
from jax.experimental import pallas as pl
from jax.experimental.pallas import tpu as pltpu

F32 = jnp.float32
BF16 = jnp.bfloat16
I32 = jnp.int32

CHUNK = 64
EPS = 1e-6
TOPK_MAX = 256
N_IDX_HEADS = 16
MAX_DISTANCE = 128
EXPERT_TOP_K = 4
SWIGLU_LIMIT = 7.0
SWIGLU_ALPHA = 1.702

LANES = 128
SUBLANES = 8
VMEM_BYTES_V7X = 64 * 1024 * 1024

QB = 128
KC = 512
SUB = KC // QB
NEG = -1e30
INT_MIN = -(2 ** 31)


def _cparams(n_axes, vmem_mb):
    return pltpu.CompilerParams(
        dimension_semantics=("arbitrary",) * n_axes,
        vmem_limit_bytes=vmem_mb * 1024 * 1024,
    )


def _pick_tile(n, max_tile, quantum):
    best = None
    t = quantum
    while t <= min(n, max_tile):
        if n % t == 0:
            best = t
        t += quantum
    assert best is not None, (n, max_tile, quantum)
    return best


def _rmsnorm_kernel(x_ref, g_ref, o_ref):
    xf = x_ref[...]
    xf = xf * lax.rsqrt(jnp.mean(xf * xf, axis=-1, keepdims=True) + EPS)
    o_ref[...] = (xf * g_ref[...]).astype(o_ref.dtype)


def _rmsnorm(x, g, out_dtype):
    m, d = x.shape
    tm = _pick_tile(m, 256, SUBLANES * 2)
    return pl.pallas_call(
        _rmsnorm_kernel,
        out_shape=jax.ShapeDtypeStruct((m, d), out_dtype),
        grid=(m // tm,),
        in_specs=[pl.BlockSpec((tm, d), lambda i: (i, 0)),
                  pl.BlockSpec((1, d), lambda i: (0, 0))],
        out_specs=pl.BlockSpec((tm, d), lambda i: (i, 0)),
        compiler_params=_cparams(1, 32),
        name="rmsnorm",
    )(x, g.reshape(1, d))


def _mm_kernel(a_ref, w_ref, o_ref, wb_ref):
    @pl.when(pl.program_id(1) == 0)
    def _():
        wb_ref[...] = w_ref[...].astype(BF16)

    o_ref[...] = jnp.dot(a_ref[...], wb_ref[...], preferred_element_type=F32)


def _matmul(a, w):
    m, k = a.shape
    n = w.shape[1]
    tm = _pick_tile(m, 640, 16)
    tn = _pick_tile(n, 512, 128)
    return pl.pallas_call(
        _mm_kernel,
        out_shape=jax.ShapeDtypeStruct((m, n), F32),
        grid=(n // tn, m // tm),
        in_specs=[pl.BlockSpec((tm, k), lambda j, i: (i, 0)),
                  pl.BlockSpec((k, tn), lambda j, i: (0, j))],
        out_specs=pl.BlockSpec((tm, tn), lambda j, i: (i, j)),
        scratch_shapes=[pltpu.VMEM((k, tn), BF16)],
        compiler_params=_cparams(2, 48),
        name="in_proj",
    )(a, w)


def _conv_prompt_kernel(u_ref, c_ref, b_ref, up_ref, cp_ref, w_ref, y_ref, st_ref):
    i = pl.program_id(0)
    tm = u_ref.shape[0]
    v = c_ref[...] * u_ref[...]
    hv = cp_ref[...] * up_ref[...]
    hv = jnp.where(i > 0, hv, 0.0)
    vp = jnp.concatenate([hv, v], axis=0)
    w = w_ref[...]
    h = SUBLANES
    y = w[0:1] * vp[h - 2:h - 2 + tm] + w[1:2] * vp[h - 1:h - 1 + tm]
    y = y + w[2:3] * v
    y_ref[...] = (b_ref[...] * y).astype(y_ref.dtype)

    @pl.when(i == pl.num_programs(0) - 1)
    def _():
        st_ref[...] = v[tm - 2:tm]


def _conv_sample_kernel(u_ref, c_ref, b_ref, s_ref, w_ref, y_ref, st_ref):
    t = u_ref.shape[0]
    v = c_ref[...] * u_ref[...]
    vp = jnp.concatenate([s_ref[...], v], axis=0)
    w = w_ref[...]
    y = w[0:1] * vp[0:t] + w[1:2] * vp[1:t + 1]
    y = y + w[2:3] * vp[2:t + 2]
    y_ref[...] = (b_ref[...] * y).astype(y_ref.dtype)
    st_ref[...] = vp[t:t + 2]


def _short_conv(proj, lay, conv_w, state, t_prompt, n_seq, s_len):
    dc = conv_w.shape[1]
    cu, cc, cb = lay["u"] // dc, lay["c"] // dc, lay["b"] // dc
    tm = _pick_tile(t_prompt, 256, 16)
    hb = tm // SUBLANES
    y_p, st_p = pl.pallas_call(
        _conv_prompt_kernel,
        out_shape=(jax.ShapeDtypeStruct((t_prompt, dc), BF16),
                   jax.ShapeDtypeStruct((2, dc), F32)),
        grid=(t_prompt // tm,),
        in_specs=[pl.BlockSpec((tm, dc), lambda i: (i, cu)),
                  pl.BlockSpec((tm, dc), lambda i: (i, cc)),
                  pl.BlockSpec((tm, dc), lambda i: (i, cb)),
                  pl.BlockSpec((SUBLANES, dc), lambda i: (jnp.maximum(i * hb - 1, 0), cu)),
                  pl.BlockSpec((SUBLANES, dc), lambda i: (jnp.maximum(i * hb - 1, 0), cc)),
                  pl.BlockSpec((3, dc), lambda i: (0, 0))],
        out_specs=(pl.BlockSpec((tm, dc), lambda i: (i, 0)),
                   pl.BlockSpec((2, dc), lambda i: (0, 0))),
        compiler_params=_cparams(1, 32),
        name="conv_prompt",
    )(proj, proj, proj, proj, proj, conv_w)

    rb = t_prompt // s_len
    y_s, st_s = pl.pallas_call(
        _conv_sample_kernel,
        out_shape=(jax.ShapeDtypeStruct((n_seq * s_len, dc), BF16),
                   jax.ShapeDtypeStruct((n_seq, 2, dc), F32)),
        grid=(n_seq,),
        in_specs=[pl.BlockSpec((s_len, dc), lambda b: (rb + b, cu)),
                  pl.BlockSpec((s_len, dc), lambda b: (rb + b, cc)),
                  pl.BlockSpec((s_len, dc), lambda b: (rb + b, cb)),
                  pl.BlockSpec((None, 2, dc), lambda b: (b, 0, 0)),
                  pl.BlockSpec((3, dc), lambda b: (0, 0))],
        out_specs=(pl.BlockSpec((s_len, dc), lambda b: (b, 0)),
                   pl.BlockSpec((None, 2, dc), lambda b: (b, 0, 0))),
        compiler_params=_cparams(1, 32),
        name="conv_sample",
    )(proj, proj, proj, state, conv_w)
    return jnp.concatenate([y_p, y_s], axis=0), st_p, st_s


def _sort_key(score):
    score = jnp.where(score == 0.0, 0.0, score)
    bits = lax.bitcast_convert_type(score, I32)
    return jnp.where(bits < 0, bits ^ 0x7FFFFFFF, bits)


def _kth_largest_key(count_ge, k_top, rows):
    zero = jnp.zeros((rows, 1), I32)
    base = jnp.where(count_ge(zero) >= k_top, zero, jnp.full((rows, 1), INT_MIN, I32))

    def body(it, base):
        cand = base | (jnp.int32(1) << (30 - it))
        return jnp.where(count_ge(cand) >= k_top, cand, base)

    return lax.fori_loop(0, 31, body, base)


def _index_scores(qi, w, ikc, n_heads, rows):
    s_all = jnp.dot(qi, ikc, preferred_element_type=F32)
    acc = None
    for h in range(n_heads):
        term = w[:, h:h + 1] * jnp.maximum(s_all[h * rows:(h + 1) * rows], 0.0)
        acc = term if acc is None else acc + term
    return acc


def _softmax_step(lm, v_g, m_ref, l_ref, acc_ref, g):
    m_old = m_ref[g]
    m_new = jnp.maximum(m_old, jnp.max(lm, axis=1, keepdims=True))
    alpha = jnp.exp(m_old - m_new)
    p = jnp.exp(lm - m_new)
    l_ref[g] = alpha * l_ref[g] + jnp.sum(p, axis=1, keepdims=True)
    acc_ref[g] = alpha * acc_ref[g] + jnp.dot(p.astype(BF16), v_g, preferred_element_type=F32)
    m_ref[g] = m_new


def _dsa_prompt_kernel(qb_ref, kb_ref, kind_ref, first_ref,
                       q_ref, iq_ref, ikw_ref, ikt_ref, ktf_ref, vf_ref,
                       kta_ref, ktb_ref, va_ref, vb_ref, nb_ref,
                       o_ref,
                       sc_ref, thr_ref, qs_ref, m_ref, l_ref, acc_ref,
                       *, k_top, n_kv, group, dh, idx_dim):
    s = pl.program_id(0)
    qb = qb_ref[s]
    kb = kb_ref[s]
    q0 = qb * QB
    scale = dh ** -0.5
    rows_g = group * QB
    row = lax.broadcasted_iota(I32, (QB, 1), 0)
    limit = q0 + (row // CHUNK + 1) * CHUNK

    @pl.when(first_ref[s] == 1)
    def _setup():
        iq = iq_ref[...]
        qi = jnp.concatenate(
            [iq[:, h * idx_dim:(h + 1) * idx_dim] for h in range(N_IDX_HEADS)], axis=0).astype(BF16)
        w = ikw_ref[:, idx_dim:idx_dim + N_IDX_HEADS] * ((N_IDX_HEADS ** -0.5) * (idx_dim ** -0.5))
        n_sw = qb // SUB + 1

        def score_body(c, carry):
            ikc = jnp.concatenate([ikt_ref[c * SUB + j] for j in range(SUB)], axis=1)
            sc = _index_scores(qi, w, ikc, N_IDX_HEADS, QB)
            pos = c * KC + lax.broadcasted_iota(I32, (QB, KC), 1)
            key = _sort_key(jnp.where(pos < limit, sc, -jnp.inf))
            for j in range(SUB):
                sc_ref[c * SUB + j] = key[:, j * QB:(j + 1) * QB]
            return carry

        lax.fori_loop(0, n_sw, score_body, 0)

        def count_ge(cand):
            def body(c, acc):
                for j in range(SUB):
                    acc = acc + (sc_ref[c * SUB + j] >= cand).astype(I32)
                return acc
            acc = lax.fori_loop(0, n_sw, body, jnp.zeros((QB, QB), I32))
            return jnp.sum(acc, axis=1, keepdims=True)

        thr_ref[...] = _kth_largest_key(count_ge, k_top, QB)

        q = q_ref[...]
        for g in range(n_kv):
            qs_ref[g] = jnp.concatenate(
                [q[:, (g * group + r) * dh:(g * group + r + 1) * dh] for r in range(group)],
                axis=0).astype(BF16)
        m_ref[...] = jnp.full(m_ref.shape, NEG, F32)
        l_ref[...] = jnp.zeros(l_ref.shape, F32)
        acc_ref[...] = jnp.zeros(acc_ref.shape, F32)

    thr = thr_ref[...]

    @pl.when(kind_ref[s] == 0)
    def _far():
        pos = kb * KC + lax.broadcasted_iota(I32, (QB, KC), 1)
        sel = jnp.concatenate([sc_ref[kb * SUB + j] >= thr for j in range(SUB)], axis=1)
        sel = sel & (pos < q0 - QB)
        for g in range(n_kv):
            lg = jnp.dot(qs_ref[g], ktf_ref[g * dh:(g + 1) * dh, :], preferred_element_type=F32) * scale
            lm = jnp.where(sel[None], lg.reshape(group, QB, KC), NEG).reshape(rows_g, KC)
            _softmax_step(lm, vf_ref[:, g * dh:(g + 1) * dh], m_ref, l_ref, acc_ref, g)

    @pl.when(kind_ref[s] == 1)
    def _window():
        w_keys = 2 * QB
        pos = q0 - QB + lax.broadcasted_iota(I32, (QB, w_keys), 1)
        sel = jnp.concatenate([sc_ref[jnp.maximum(qb - 1, 0)] >= thr, sc_ref[qb] >= thr], axis=1)
        sel = sel & (pos >= 0) & (pos < limit)
        kt = jnp.concatenate([kta_ref[...], ktb_ref[...]], axis=1)
        v = jnp.concatenate([va_ref[...], vb_ref[...]], axis=0)
        outs = []
        for g in range(n_kv):
            lg = jnp.dot(qs_ref[g], kt[g * dh:(g + 1) * dh, :], preferred_element_type=F32) * scale
            lg = lg.reshape(group, QB, w_keys) + nb_ref[g * group:(g + 1) * group]
            lm = jnp.where(sel[None], lg, NEG).reshape(rows_g, w_keys)
            _softmax_step(lm, v[:, g * dh:(g + 1) * dh], m_ref, l_ref, acc_ref, g)
            out_g = acc_ref[g] / l_ref[g]
            outs += [out_g[r * QB:(r + 1) * QB] for r in range(group)]
        o_ref[...] = jnp.concatenate(outs, axis=1).astype(o_ref.dtype)


def _t5_bucket(rel, n_buckets):
    nb = n_buckets // 2
    max_exact = nb // 2
    n = jnp.abs(rel)
    nf = jnp.maximum(n, 1).astype(F32)
    large = max_exact + (jnp.log(nf / max_exact) / math.log(MAX_DISTANCE / max_exact)
                         * (nb - max_exact)).astype(I32)
    large = jnp.minimum(large, nb - 1)
    return jnp.where(rel > 0, nb, 0) + jnp.where(n < max_exact, n, large)


def _dsa_prompt(proj, lay, ik_t3, k_t, v_bf, rel_bias, t, n_kv, dh, d_attn, idx_dim):
    n_heads = d_attn // dh
    group = n_heads // n_kv
    d_kv = n_kv * dh
    nqb = t // QB
    k_top = min(TOPK_MAX, t // 4)
    n_buckets = rel_bias.shape[0]

    rel = (jnp.arange(2 * QB, dtype=I32)[None, :] - QB) - jnp.arange(QB, dtype=I32)[:, None]
    nb = rel_bias[_t5_bucket(rel, n_buckets)].astype(F32)
    nb = jnp.moveaxis(nb, -1, 0) - rel_bias[n_buckets // 2 - 1].astype(F32)[:, None, None]

    s_qb, s_kb, s_kind, s_first = [], [], [], []
    for qb in range(nqb):
        n_far = -(-max(qb - 1, 0) * QB // KC)
        for kb in range(n_far):
            s_qb.append(qb); s_kb.append(kb); s_kind.append(0); s_first.append(int(kb == 0))
        s_qb.append(qb); s_kb.append(max(n_far - 1, 0)); s_kind.append(1); s_first.append(int(n_far == 0))
    tabs = [jnp.asarray(np.asarray(a, np.int32)) for a in (s_qb, s_kb, s_kind, s_first)]
    n_steps = len(s_qb)

    cq = lay["q"] // d_attn
    ciq = lay["iq"] // (N_IDX_HEADS * idx_dim)
    cikw = lay["ikw"] // LANES
    prev = lambda s, qb, kb, kd, fs: jnp.maximum(qb[s] - 1, 0)

    grid_spec = pltpu.PrefetchScalarGridSpec(
        num_scalar_prefetch=4,
        grid=(n_steps,),
        in_specs=[
            pl.BlockSpec((QB, d_attn), lambda s, qb, kb, kd, fs: (qb[s], cq)),
            pl.BlockSpec((QB, N_IDX_HEADS * idx_dim), lambda s, qb, kb, kd, fs: (qb[s], ciq)),
            pl.BlockSpec((QB, LANES), lambda s, qb, kb, kd, fs: (qb[s], cikw)),
            pl.BlockSpec((nqb, idx_dim, QB), lambda s, qb, kb, kd, fs: (0, 0, 0)),
            pl.BlockSpec((d_kv, KC), lambda s, qb, kb, kd, fs: (0, kb[s])),
            pl.BlockSpec((KC, d_kv), lambda s, qb, kb, kd, fs: (kb[s], 0)),
            pl.BlockSpec((d_kv, QB), lambda s, qb, kb, kd, fs: (0, prev(s, qb, kb, kd, fs))),
            pl.BlockSpec((d_kv, QB), lambda s, qb, kb, kd, fs: (0, qb[s])),
            pl.BlockSpec((QB, d_kv), lambda s, qb, kb, kd, fs: (prev(s, qb, kb, kd, fs), 0)),
            pl.BlockSpec((QB, d_kv), lambda s, qb, kb, kd, fs: (qb[s], 0)),
            pl.BlockSpec((n_heads, QB, 2 * QB), lambda s, qb, kb, kd, fs: (0, 0, 0)),
        ],
        out_specs=pl.BlockSpec((QB, d_attn), lambda s, qb, kb, kd, fs: (qb[s], 0)),
        scratch_shapes=[
            pltpu.VMEM((nqb, QB, QB), I32),
            pltpu.VMEM((QB, 1), I32),
            pltpu.VMEM((n_kv, group * QB, dh), BF16),
            pltpu.VMEM((n_kv, group * QB, 1), F32),
            pltpu.VMEM((n_kv, group * QB, 1), F32),
            pltpu.VMEM((n_kv, group * QB, dh), F32),
        ],
    )
    kern = functools.partial(_dsa_prompt_kernel, k_top=k_top, n_kv=n_kv, group=group, dh=dh,
                             idx_dim=idx_dim)
    return pl.pallas_call(
        kern,
        out_shape=jax.ShapeDtypeStruct((t, d_attn), BF16),
        grid_spec=grid_spec,
        compiler_params=_cparams(1, 48),
        name="dsa_prompt",
    )(*tabs, proj, proj, proj, ik_t3, k_t, v_bf, k_t, k_t, v_bf, v_bf, nb)


def _dsa_sample_kernel(q_ref, iq_ref, ikw_ref, ikt_ref, kt_ref, v_ref, bias_ref, o_ref,
                       *, k_top, n_kv, group, dh, idx_dim, past, n_keys):
    s_len = q_ref.shape[0]
    lp = kt_ref.shape[1]
    scale = dh ** -0.5
    iq = iq_ref[...]
    qi = jnp.concatenate(
        [iq[:, h * idx_dim:(h + 1) * idx_dim] for h in range(N_IDX_HEADS)], axis=0).astype(BF16)
    w = ikw_ref[:, idx_dim:idx_dim + N_IDX_HEADS] * ((N_IDX_HEADS ** -0.5) * (idx_dim ** -0.5))
    sc = _index_scores(qi, w, ikt_ref[...], N_IDX_HEADS, s_len)
    row = lax.broadcasted_iota(I32, (s_len, 1), 0)
    limit = jnp.minimum(((past + row) // CHUNK + 1) * CHUNK, n_keys)
    pos = lax.broadcasted_iota(I32, (s_len, lp), 1)
    allowed = pos < limit
    key = _sort_key(jnp.where(allowed, sc, -jnp.inf))

    def count_ge(cand):
        return jnp.sum((key >= cand).astype(I32), axis=1, keepdims=True)

    thr = _kth_largest_key(count_ge, k_top, s_len)
    sel = (key >= thr) & allowed
    q = q_ref[...]
    outs = []
    for g in range(n_kv):
        qg = jnp.concatenate(
            [q[:, (g * group + r) * dh:(g * group + r + 1) * dh] for r in range(group)],
            axis=0).astype(BF16)
        lg = jnp.dot(qg, kt_ref[g * dh:(g + 1) * dh, :], preferred_element_type=F32) * scale
        lg = lg.reshape(group, s_len, lp) + bias_ref[g * group:(g + 1) * group]
        lm = jnp.where(sel[None], lg, NEG).reshape(group * s_len, lp)
        p = jnp.exp(lm - jnp.max(lm, axis=1, keepdims=True))
        den = jnp.sum(p, axis=1, keepdims=True)
        out_g = jnp.dot(p.astype(BF16), v_ref[:, g * dh:(g + 1) * dh], preferred_element_type=F32) / den
        outs += [out_g[r * s_len:(r + 1) * s_len] for r in range(group)]
    o_ref[...] = jnp.concatenate(outs, axis=1).astype(o_ref.dtype)


def _dsa_sample(proj, lay, ik_t, k_t, v_bf, rel_bias, row0, n_seq, s_len, past, n_kv, dh, d_attn,
                idx_dim):
    n_heads = d_attn // dh
    group = n_heads // n_kv
    d_kv = n_kv * dh
    lp = k_t.shape[2]
    n_keys = past + s_len
    k_top = min(TOPK_MAX, n_keys // 4)
    rel = jnp.arange(lp, dtype=I32)[None, :] - (past + jnp.arange(s_len, dtype=I32))[:, None]
    bias = jnp.moveaxis(rel_bias[_t5_bucket(rel, rel_bias.shape[0])].astype(F32), -1, 0)
    rb = row0 // s_len
    cq = lay["q"] // d_attn
    ciq = lay["iq"] // (N_IDX_HEADS * idx_dim)
    cikw = lay["ikw"] // LANES
    kern = functools.partial(_dsa_sample_kernel, k_top=k_top, n_kv=n_kv, group=group, dh=dh,
                             idx_dim=idx_dim, past=past, n_keys=n_keys)
    return pl.pallas_call(
        kern,
        out_shape=jax.ShapeDtypeStruct((n_seq * s_len, d_attn), BF16),
        grid=(n_seq,),
        in_specs=[
            pl.BlockSpec((s_len, d_attn), lambda b: (rb + b, cq)),
            pl.BlockSpec((s_len, N_IDX_HEADS * idx_dim), lambda b: (rb + b, ciq)),
            pl.BlockSpec((s_len, LANES), lambda b: (rb + b, cikw)),
            pl.BlockSpec((None, idx_dim, lp), lambda b: (b, 0, 0)),
            pl.BlockSpec((None, d_kv, lp), lambda b: (b, 0, 0)),
            pl.BlockSpec((None, lp, d_kv), lambda b: (b, 0, 0)),
            pl.BlockSpec((n_heads, s_len, lp), lambda b: (0, 0, 0)),
        ],
        out_specs=pl.BlockSpec((s_len, d_attn), lambda b: (b, 0)),
        compiler_params=_cparams(1, 32),
        name="dsa_sample",
    )(proj, proj, proj, ik_t, k_t, v_bf, bias)


def _merge_kernel(yc_ref, ya_ref, gc_ref, ga_ref, wc_ref, wa_ref, o_ref, wcb_ref, wab_ref):
    @pl.when(pl.program_id(1) == 0)
    def _():
        wcb_ref[...] = wc_ref[...].astype(BF16)
        wab_ref[...] = wa_ref[...].astype(BF16)

    bc = jnp.dot(yc_ref[...], wcb_ref[...], preferred_element_type=F32)
    ba = jnp.dot(ya_ref[...], wab_ref[...], preferred_element_type=F32)
    merged = jax.nn.sigmoid(gc_ref[...]) * bc + jax.nn.sigmoid(ga_ref[...]) * ba
    o_ref[...] = merged.astype(o_ref.dtype)


def _merge(y_conv, y_attn, proj, lay, w_bc, w_ba):
    m, dc = y_conv.shape
    da = y_attn.shape[1]
    d = w_bc.shape[1]
    tm = _pick_tile(m, 640, 16)
    tn = _pick_tile(d, 512, 128)
    cgc, cga = lay["gc"] // tn, lay["ga"] // tn
    return pl.pallas_call(
        _merge_kernel,
        out_shape=jax.ShapeDtypeStruct((m, d), BF16),
        grid=(d // tn, m // tm),
        in_specs=[pl.BlockSpec((tm, dc), lambda j, i: (i, 0)),
                  pl.BlockSpec((tm, da), lambda j, i: (i, 0)),
                  pl.BlockSpec((tm, tn), lambda j, i: (i, cgc + j)),
                  pl.BlockSpec((tm, tn), lambda j, i: (i, cga + j)),
                  pl.BlockSpec((dc, tn), lambda j, i: (0, j)),
                  pl.BlockSpec((da, tn), lambda j, i: (0, j))],
        out_specs=pl.BlockSpec((tm, tn), lambda j, i: (i, j)),
        scratch_shapes=[pltpu.VMEM((dc, tn), BF16), pltpu.VMEM((da, tn), BF16)],
        compiler_params=_cparams(2, 48),
        name="merge",
    )(y_conv, y_attn, proj, proj, w_bc, w_ba)


def _out_kernel(a_ref, x_ref, w_ref, o_ref, wb_ref):
    @pl.when(pl.program_id(1) == 0)
    def _():
        wb_ref[...] = w_ref[...].astype(BF16)

    o_ref[...] = x_ref[...] + jnp.dot(a_ref[...], wb_ref[...], preferred_element_type=F32)


def _out_proj(merged, x, w_out):
    m, k = merged.shape
    n = w_out.shape[1]
    tm = _pick_tile(m, 640, 16)
    tn = _pick_tile(n, 512, 128)
    return pl.pallas_call(
        _out_kernel,
        out_shape=jax.ShapeDtypeStruct((m, n), F32),
        grid=(n // tn, m // tm),
        in_specs=[pl.BlockSpec((tm, k), lambda j, i: (i, 0)),
                  pl.BlockSpec((tm, tn), lambda j, i: (i, j)),
                  pl.BlockSpec((k, tn), lambda j, i: (0, j))],
        out_specs=pl.BlockSpec((tm, tn), lambda j, i: (i, j)),
        scratch_shapes=[pltpu.VMEM((k, tn), BF16)],
        compiler_params=_cparams(2, 48),
        name="out_proj",
    )(merged, x, w_out)


def _split_bf16(x):
    hi = x.astype(BF16)
    lo = (x - hi.astype(F32)).astype(BF16)
    return hi, lo


def _router_kernel(x_ref, g_ref, rw_ref, rb_ref, h_ref, idx_ref, gate_ref):
    xf = x_ref[...]
    xf = xf * lax.rsqrt(jnp.mean(xf * xf, axis=-1, keepdims=True) + EPS)
    h = xf * g_ref[...]
    h_ref[...] = h
    h_hi, h_lo = _split_bf16(h)
    w_hi, w_lo = _split_bf16(rw_ref[...])
    logits = (jnp.dot(h_hi, w_hi, preferred_element_type=F32)
              + (jnp.dot(h_hi, w_lo, preferred_element_type=F32)
                 + jnp.dot(h_lo, w_hi, preferred_element_type=F32)))
    logits = logits + rb_ref[...]
    n_e = logits.shape[1]
    lane = lax.broadcasted_iota(I32, logits.shape, 1)
    vals, idxs = [], []
    cur = logits
    for _ in range(EXPERT_TOP_K):
        mx = jnp.max(cur, axis=1, keepdims=True)
        ix = jnp.min(jnp.where(cur == mx, lane, n_e), axis=1, keepdims=True)
        vals.append(mx)
        idxs.append(ix)
        cur = jnp.where(lane == ix, -jnp.inf, cur)
    top = jnp.concatenate(vals, axis=1)
    e = jnp.exp(top - vals[0])
    gate_ref[...] = e / jnp.sum(e, axis=1, keepdims=True)
    idx_ref[...] = jnp.concatenate(idxs, axis=1)


def _router(x1, g_ffn, router_w, router_b):
    m, d = x1.shape
    n_e = router_w.shape[1]
    tm = _pick_tile(m, 256, SUBLANES)
    return pl.pallas_call(
        _router_kernel,
        out_shape=(jax.ShapeDtypeStruct((m, d), F32),
                   jax.ShapeDtypeStruct((m, EXPERT_TOP_K), I32),
                   jax.ShapeDtypeStruct((m, EXPERT_TOP_K), F32)),
        grid=(m // tm,),
        in_specs=[pl.BlockSpec((tm, d), lambda i: (i, 0)),
                  pl.BlockSpec((1, d), lambda i: (0, 0)),
                  pl.BlockSpec((d, n_e), lambda i: (0, 0)),
                  pl.BlockSpec((1, n_e), lambda i: (0, 0))],
        out_specs=(pl.BlockSpec((tm, d), lambda i: (i, 0)),
                   pl.BlockSpec((tm, EXPERT_TOP_K), lambda i: (i, 0)),
                   pl.BlockSpec((tm, EXPERT_TOP_K), lambda i: (i, 0))),
        compiler_params=_cparams(1, 48),
        name="router",
    )(x1, g_ffn.reshape(1, d), router_w, router_b.reshape(1, n_e))


def _row_copy(src_hbm, buf, sem, src_row, dst_row):
    return pltpu.make_async_copy(src_hbm.at[src_row], buf.at[dst_row], sem)


def _gather_kernel(idx_hbm, src_hbm, o_ref, idx_smem, buf, sem_idx, sem):
    i = pl.program_id(0)
    n = buf.shape[0]
    cp = pltpu.make_async_copy(idx_hbm.at[i], idx_smem, sem_idx)
    cp.start()
    cp.wait()

    def issue(r, c):
        _row_copy(src_hbm, buf, sem, idx_smem[r], r).start()
        return c

    lax.fori_loop(0, n, issue, 0)

    def drain(r, c):
        _row_copy(src_hbm, buf, sem, 0, r).wait()
        return c

    lax.fori_loop(0, n, drain, 0)
    o_ref[...] = buf[...].astype(o_ref.dtype)


def _gather_rows(src, idx, rows_per_step):
    n_rows = idx.shape[0]
    d = src.shape[1]
    nb = n_rows // rows_per_step
    return pl.pallas_call(
        _gather_kernel,
        out_shape=jax.ShapeDtypeStruct((n_rows, d), BF16),
        grid=(nb,),
        in_specs=[pl.BlockSpec(memory_space=pl.ANY), pl.BlockSpec(memory_space=pl.ANY)],
        out_specs=pl.BlockSpec((rows_per_step, d), lambda i: (i, 0)),
        scratch_shapes=[pltpu.SMEM((rows_per_step,), I32),
                        pltpu.VMEM((rows_per_step, d), src.dtype),
                        pltpu.SemaphoreType.DMA(()),
                        pltpu.SemaphoreType.DMA(())],
        compiler_params=_cparams(1, 32),
        name="moe_gather",
    )(idx.reshape(nb, rows_per_step), src)


def _expert_changed(be_ref, i):
    return (i == 0) | (be_ref[i] != be_ref[jnp.maximum(i - 1, 0)])


def _up_kernel(be_ref, nv_ref, x_ref, wg_ref, wl_ref, bg_ref, bl_ref, o_ref, wgb_ref, wlb_ref):
    i = pl.program_id(1)

    @pl.when(_expert_changed(be_ref, i))
    def _():
        wgb_ref[...] = wg_ref[...].astype(BF16)
        wlb_ref[...] = wl_ref[...].astype(BF16)

    @pl.when(i < nv_ref[0])
    def _():
        x = x_ref[...]
        hg = jnp.dot(x, wgb_ref[...], preferred_element_type=F32) + bg_ref[...]
        hl = jnp.dot(x, wlb_ref[...], preferred_element_type=F32) + bl_ref[...]
        x_glu = jnp.minimum(hg, SWIGLU_LIMIT)
        x_lin = jnp.clip(hl, -SWIGLU_LIMIT, SWIGLU_LIMIT)
        act = x_glu * jax.nn.sigmoid(SWIGLU_ALPHA * x_glu) * (x_lin + 1.0)
        o_ref[...] = act.astype(o_ref.dtype)

    @pl.when(i >= nv_ref[0])
    def _():
        o_ref[...] = jnp.zeros(o_ref.shape, o_ref.dtype)


def _expert_up(xs, blk_expert, n_valid, w_up, b_up, tm):
    n_rows, d = xs.shape
    n_e, _, f2 = w_up.shape
    f = f2 // 2
    tf = _pick_tile(f, 256, 128)
    nf = f // tf
    nblk = n_rows // tm
    grid_spec = pltpu.PrefetchScalarGridSpec(
        num_scalar_prefetch=2,
        grid=(nf, nblk),
        in_specs=[
            pl.BlockSpec((tm, d), lambda j, i, be, nv: (i, 0)),
            pl.BlockSpec((None, d, tf), lambda j, i, be, nv: (be[i], 0, j)),
            pl.BlockSpec((None, d, tf), lambda j, i, be, nv: (be[i], 0, nf + j)),
            pl.BlockSpec((None, 1, tf), lambda j, i, be, nv: (be[i], 0, j)),
            pl.BlockSpec((None, 1, tf), lambda j, i, be, nv: (be[i], 0, nf + j)),
        ],
        out_specs=pl.BlockSpec((tm, tf), lambda j, i, be, nv: (i, j)),
        scratch_shapes=[pltpu.VMEM((d, tf), BF16), pltpu.VMEM((d, tf), BF16)],
    )
    return pl.pallas_call(
        _up_kernel,
        out_shape=jax.ShapeDtypeStruct((n_rows, f), BF16),
        grid_spec=grid_spec,
        compiler_params=_cparams(2, 48),
        name="moe_up",
    )(blk_expert, n_valid, xs, w_up, w_up, b_up.reshape(n_e, 1, f2), b_up.reshape(n_e, 1, f2))


def _down_kernel(be_ref, nv_ref, a_ref, w_ref, b_ref, g_ref, o_ref, wb_ref):
    i = pl.program_id(1)

    @pl.when(_expert_changed(be_ref, i))
    def _():
        wb_ref[...] = w_ref[...].astype(BF16)

    @pl.when(i < nv_ref[0])
    def _():
        y = jnp.dot(a_ref[...], wb_ref[...], preferred_element_type=F32) + b_ref[...]
        o_ref[...] = y * g_ref[...]

    @pl.when(i >= nv_ref[0])
    def _():
        o_ref[...] = jnp.zeros(o_ref.shape, o_ref.dtype)


def _expert_down(act, blk_expert, n_valid, w_down, b_down, gate_rows, tm):
    n_rows, f = act.shape
    n_e, _, d = w_down.shape
    tn = _pick_tile(d, 512, 128)
    nblk = n_rows // tm
    grid_spec = pltpu.PrefetchScalarGridSpec(
        num_scalar_prefetch=2,
        grid=(d // tn, nblk),
        in_specs=[
            pl.BlockSpec((tm, f), lambda j, i, be, nv: (i, 0)),
            pl.BlockSpec((None, f, tn), lambda j, i, be, nv: (be[i], 0, j)),
            pl.BlockSpec((None, 1, tn), lambda j, i, be, nv: (be[i], 0, j)),
            pl.BlockSpec((tm, 1), lambda j, i, be, nv: (i, 0)),
        ],
        out_specs=pl.BlockSpec((tm, tn), lambda j, i, be, nv: (i, j)),
        scratch_shapes=[pltpu.VMEM((f, tn), BF16)],
    )
    return pl.pallas_call(
        _down_kernel,
        out_shape=jax.ShapeDtypeStruct((n_rows, d), F32),
        grid_spec=grid_spec,
        compiler_params=_cparams(2, 48),
        name="moe_down",
    )(blk_expert, n_valid, act, w_down, b_down.reshape(n_e, 1, d), gate_rows)


def _combine_kernel(pos_hbm, y_hbm, x_ref, g_ref, o_ref, pos_smem, buf, sem_idx, sem):
    i = pl.program_id(0)
    tb = x_ref.shape[0]
    n = pos_smem.shape[0]
    cp = pltpu.make_async_copy(pos_hbm.at[i], pos_smem, sem_idx)
    cp.start()
    cp.wait()

    def issue(r, c):
        _row_copy(y_hbm, buf, sem, pos_smem[r], r).start()
        return c

    lax.fori_loop(0, n, issue, 0)

    def drain(r, c):
        _row_copy(y_hbm, buf, sem, 0, r).wait()
        return c

    lax.fori_loop(0, n, drain, 0)
    moe = buf[0:tb]
    for j in range(1, EXPERT_TOP_K):
        moe = moe + buf[j * tb:(j + 1) * tb]
    xf = x_ref[...] + moe
    xf = xf * lax.rsqrt(jnp.mean(xf * xf, axis=-1, keepdims=True) + EPS)
    o_ref[...] = xf * g_ref[...]


def _combine(y_rows, pos, x1, g_final, tb):
    m, d = x1.shape
    nb = m // tb
    pos_blocks = pos.reshape(nb, tb, EXPERT_TOP_K).transpose(0, 2, 1).reshape(nb, EXPERT_TOP_K * tb)
    return pl.pallas_call(
        _combine_kernel,
        out_shape=jax.ShapeDtypeStruct((m, d), F32),
        grid=(nb,),
        in_specs=[pl.BlockSpec(memory_space=pl.ANY), pl.BlockSpec(memory_space=pl.ANY),
                  pl.BlockSpec((tb, d), lambda i: (i, 0)),
                  pl.BlockSpec((1, d), lambda i: (0, 0))],
        out_specs=pl.BlockSpec((tb, d), lambda i: (i, 0)),
        scratch_shapes=[pltpu.SMEM((EXPERT_TOP_K * tb,), I32),
                        pltpu.VMEM((EXPERT_TOP_K * tb, d), F32),
                        pltpu.SemaphoreType.DMA(()),
                        pltpu.SemaphoreType.DMA(())],
        compiler_params=_cparams(1, 32),
        name="moe_combine",
    )(pos_blocks, y_rows, x1, g_final.reshape(1, d))


def _routing_tables(top_idx, gates, n_experts, tm):
    t = top_idx.shape[0]
    n = t * EXPERT_TOP_K
    e_flat = top_idx.reshape(n)
    g_flat = gates.reshape(n)
    order = jnp.argsort(e_flat)
    e_sorted = e_flat[order]
    counts = jnp.bincount(e_flat, length=n_experts).astype(I32)
    starts = jnp.cumsum(counts) - counts
    padded = (counts + tm - 1) // tm * tm
    padded_end = jnp.cumsum(padded)
    padded_start = padded_end - padded
    dest = padded_start[e_sorted] + (jnp.arange(n, dtype=I32) - starts[e_sorted])
    n_blocks = -(-n // tm) + n_experts
    n_rows = n_blocks * tm
    tok_flat = jnp.arange(n, dtype=I32) // EXPERT_TOP_K
    buf_tok = jnp.zeros((n_rows,), I32).at[dest].set(tok_flat[order])
    buf_gate = jnp.zeros((n_rows,), F32).at[dest].set(g_flat[order])
    pos = jnp.zeros((n,), I32).at[order].set(dest).reshape(t, EXPERT_TOP_K)
    blk_start = jnp.arange(n_blocks, dtype=I32) * tm
    blk_expert = jnp.minimum(jnp.searchsorted(padded_end, blk_start, side="right"),
                             n_experts - 1).astype(I32)
    n_valid = (padded_end[-1] // tm).astype(I32).reshape(1)
    return buf_tok, buf_gate, pos, blk_expert, n_valid


def _proj_layout(d, dc, d_attn, d_kv, idx_dim):
    iq_w = N_IDX_HEADS * idx_dim
    assert idx_dim + N_IDX_HEADS <= LANES
    segs = [("gc", d), ("ga", d), ("u", dc), ("c", dc), ("b", dc), ("q", d_attn),
            ("iq", iq_w), ("k", d_kv), ("v", d_kv), ("ikw", LANES)]
    segs.sort(key=lambda s: -s[1])
    lay, off = {}, 0
    for name, width in segs:
        assert off % width == 0, (name, off, width)
        lay[name] = off
        off += width
    lay["total"] = -(-off // 512) * 512
    return lay


def _relayout_w_in(w, lay, dc, d_attn, d_kv, idx_dim, d):
    widths = (dc, dc, dc, d_attn, d_kv, d_kv, N_IDX_HEADS * idx_dim, idx_dim, N_IDX_HEADS, d, d)
    names = ("u", "c", "b", "q", "k", "v", "iq", "ik", "iw", "gc", "ga")
    src, o = {}, 0
    for nme, wd in zip(names, widths):
        src[nme] = (o, wd)
        o += wd
    assert o == w.shape[1]
    out = jnp.zeros((w.shape[0], lay["total"]), w.dtype)
    for nme in ("gc", "ga", "u", "c", "b", "q", "iq", "k", "v"):
        s0, wd = src[nme]
        out = lax.dynamic_update_slice(out, w[:, s0:s0 + wd], (0, lay[nme]))
    s0 = src["ik"][0]
    out = lax.dynamic_update_slice(out, w[:, s0:s0 + idx_dim + N_IDX_HEADS], (0, lay["ikw"]))
    return out


def kernel(x_prompt, x_sample, cache_k, cache_v, cache_idx_k, state_conv, g_mix, w_in, conv_w,
           w_branch_conv, w_branch_attn, w_out, rel_bias, g_ffn, router_w, router_b, w_up, b_up,
           w_down, b_down, g_final):
    depth = w_in.shape[0]
    assert depth == 1, "one trunk layer"
    bp, t, d = x_prompt.shape
    assert bp == 1
    n_seq, s_len, _ = x_sample.shape
    past, n_kv, dh = cache_k.shape[2:5]
    idx_dim = cache_idx_k.shape[-1]
    dc = conv_w.shape[-1]
    d_attn = w_branch_attn.shape[1]
    d_kv = n_kv * dh
    n_experts = router_w.shape[-1]
    assert t % KC == 0 and s_len % 16 == 0

    lay = _proj_layout(d, dc, d_attn, d_kv, idx_dim)
    w_in_r = _relayout_w_in(w_in[0], lay, dc, d_attn, d_kv, idx_dim, d)

    x_all = jnp.concatenate([x_prompt.reshape(t, d), x_sample.reshape(n_seq * s_len, d)], axis=0)
    h = _rmsnorm(x_all, g_mix[0], BF16)
    proj = _matmul(h, w_in_r)

    k_new = proj[:, lay["k"]:lay["k"] + d_kv]
    v_new = proj[:, lay["v"]:lay["v"] + d_kv]
    ik_new = proj[:, lay["ikw"]:lay["ikw"] + idx_dim]

    y_conv, st_p, st_s = _short_conv(proj, lay, conv_w[0], state_conv[0], t, n_seq, s_len)

    k_t = k_new[:t].T.astype(BF16)
    v_bf = v_new[:t].astype(BF16)
    ik_t3 = ik_new[:t].reshape(t // QB, QB, idx_dim).transpose(0, 2, 1).astype(BF16)
    ya_p = _dsa_prompt(proj, lay, ik_t3, k_t, v_bf, rel_bias, t, n_kv, dh, d_attn, idx_dim)

    n_keys = past + s_len
    lp = -(-n_keys // LANES) * LANES

    def with_cache(cache, new, width):
        full = jnp.concatenate([cache.reshape(n_seq, past, width),
                                new[t:].reshape(n_seq, s_len, width)], axis=1)
        return jnp.pad(full, ((0, 0), (0, lp - n_keys), (0, 0))).astype(BF16)

    k_s = with_cache(cache_k[0], k_new, d_kv)
    v_s = with_cache(cache_v[0], v_new, d_kv)
    ik_s = with_cache(cache_idx_k[0], ik_new, idx_dim)
    ya_s = _dsa_sample(proj, lay, ik_s.transpose(0, 2, 1), k_s.transpose(0, 2, 1), v_s, rel_bias,
                       t, n_seq, s_len, past, n_kv, dh, d_attn, idx_dim)
    y_attn = jnp.concatenate([ya_p, ya_s], axis=0)

    merged = _merge(y_conv, y_attn, proj, lay, w_branch_conv[0], w_branch_attn[0])
    x1 = _out_proj(merged, x_all, w_out[0])

    h2, top_idx, gates = _router(x1, g_ffn[0], router_w[0], router_b[0])
    tm_e = 512 if (x_all.shape[0] * EXPERT_TOP_K) >= 16 * 512 else 128
    buf_tok, buf_gate, pos, blk_expert, n_valid = _routing_tables(top_idx, gates, n_experts, tm_e)
    xs = _gather_rows(h2, buf_tok, 128)
    act = _expert_up(xs, blk_expert, n_valid, w_up[0], b_up[0], tm_e)
    y_rows = _expert_down(act, blk_expert, n_valid, w_down[0], b_down[0], buf_gate.reshape(-1, 1), tm_e)
    tb = _pick_tile(x_all.shape[0], 64, SUBLANES)
    y_all = _combine(y_rows, pos, x1, g_final, tb)

    y_prompt = y_all[:t].reshape(1, t, d)
    y_sample = y_all[t:].reshape(n_seq, s_len, d)
    return (y_prompt, y_sample,
            k_new[:t].reshape(1, 1, t, n_kv, dh), v_new[:t].reshape(1, 1, t, n_kv, dh),
            ik_new[:t].reshape(1, 1, t, idx_dim), st_p.reshape(1, 1, 2, dc),
            k_new[t:].reshape(1, n_seq, s_len, n_kv, dh), v_new[t:].reshape(1, n_seq, s_len, n_kv, dh),
            ik_new[t:].reshape(1, n_seq, s_len, idx_dim), st_s.reshape(1, n_seq, 2, dc))
```

```python
import functools
import math

import numpy as np
import jax
import jax.numpy as jnp
from jax import lax
from jax.experimental import pallas as pl
from jax.experimental.pallas import tpu as pltpu

F32 = jnp.float32
BF16 = jnp.bfloat16
I32 = jnp.int32

CHUNK = 64
EPS = 1e-6
TOPK_MAX = 256
N_IDX_HEADS = 16
MAX_DISTANCE = 128
EXPERT_TOP_K = 4
SWIGLU_LIMIT = 7.0
SWIGLU_ALPHA = 1.702

LANES = 128
SUBLANES = 8
VMEM_BYTES_V7X = 64 * 1024 * 1024

QB = 128
KC = 512
SUB = KC // QB
SLAB = 32
LOG2E = 1.4426950408889634
NEG = -1e30
INT_MIN = -(2 ** 31)
INT_MAX = 2 ** 31 - 1


def _cparams(n_axes, vmem_mb):
    return pltpu.CompilerParams(
        dimension_semantics=("arbitrary",) * n_axes,
        vmem_limit_bytes=vmem_mb * 1024 * 1024,
    )


def _pick_tile(n, max_tile, quantum):
    best = None
    t = quantum
    while t <= min(n, max_tile):
        if n % t == 0:
            best = t
        t += quantum
    assert best is not None, (n, max_tile, quantum)
    return best


def _rmsnorm_kernel(x_ref, g_ref, o_ref):
    xf = x_ref[...]
    xf = xf * lax.rsqrt(jnp.mean(xf * xf, axis=-1, keepdims=True) + EPS)
    o_ref[...] = (xf * g_ref[...]).astype(o_ref.dtype)


def _rmsnorm(x, g, out_dtype):
    m, d = x.shape
    tm = _pick_tile(m, 256, SUBLANES * 2)
    return pl.pallas_call(
        _rmsnorm_kernel,
        out_shape=jax.ShapeDtypeStruct((m, d), out_dtype),
        grid=(m // tm,),
        in_specs=[pl.BlockSpec((tm, d), lambda i: (i, 0)),
                  pl.BlockSpec((1, d), lambda i: (0, 0))],
        out_specs=pl.BlockSpec((tm, d), lambda i: (i, 0)),
        compiler_params=_cparams(1, 32),
        name="rmsnorm",
    )(x, g.reshape(1, d))


def _mm_kernel(a_ref, w_ref, o_ref, wb_ref):
    @pl.when(pl.program_id(1) == 0)
    def _():
        wb_ref[...] = w_ref[...].astype(BF16)

    o_ref[...] = jnp.dot(a_ref[...], wb_ref[...], preferred_element_type=F32)


def _matmul(a, w):
    m, k = a.shape
    n = w.shape[1]
    tm = _pick_tile(m, 640, 16)
    tn = _pick_tile(n, 512, 128)
    return pl.pallas_call(
        _mm_kernel,
        out_shape=jax.ShapeDtypeStruct((m, n), F32),
        grid=(n // tn, m // tm),
        in_specs=[pl.BlockSpec((tm, k), lambda j, i: (i, 0)),
                  pl.BlockSpec((k, tn), lambda j, i: (0, j))],
        out_specs=pl.BlockSpec((tm, tn), lambda j, i: (i, j)),
        scratch_shapes=[pltpu.VMEM((k, tn), BF16)],
        compiler_params=_cparams(2, 48),
        name="in_proj",
    )(a, w)


def _conv_prompt_kernel(u_ref, c_ref, b_ref, up_ref, cp_ref, w_ref, y_ref, st_ref):
    i = pl.program_id(0)
    tm = u_ref.shape[0]
    v = c_ref[...] * u_ref[...]
    hv = cp_ref[...] * up_ref[...]
    hv = jnp.where(i > 0, hv, 0.0)
    vp = jnp.concatenate([hv, v], axis=0)
    w = w_ref[...]
    h = SUBLANES
    y = w[0:1] * vp[h - 2:h - 2 + tm] + w[1:2] * vp[h - 1:h - 1 + tm]
    y = y + w[2:3] * v
    y_ref[...] = (b_ref[...] * y).astype(y_ref.dtype)

    @pl.when(i == pl.num_programs(0) - 1)
    def _():
        st_ref[...] = v[tm - 2:tm]


def _conv_sample_kernel(u_ref, c_ref, b_ref, s_ref, w_ref, y_ref, st_ref):
    t = u_ref.shape[0]
    v = c_ref[...] * u_ref[...]
    vp = jnp.concatenate([s_ref[...], v], axis=0)
    w = w_ref[...]
    y = w[0:1] * vp[0:t] + w[1:2] * vp[1:t + 1]
    y = y + w[2:3] * vp[2:t + 2]
    y_ref[...] = (b_ref[...] * y).astype(y_ref.dtype)
    st_ref[...] = vp[t:t + 2]


def _short_conv(proj, lay, conv_w, state, t_prompt, n_seq, s_len):
    dc = conv_w.shape[1]
    cu, cc, cb = lay["u"] // dc, lay["c"] // dc, lay["b"] // dc
    tm = _pick_tile(t_prompt, 256, 16)
    hb = tm // SUBLANES
    y_p, st_p = pl.pallas_call(
        _conv_prompt_kernel,
        out_shape=(jax.ShapeDtypeStruct((t_prompt, dc), BF16),
                   jax.ShapeDtypeStruct((2, dc), F32)),
        grid=(t_prompt // tm,),
        in_specs=[pl.BlockSpec((tm, dc), lambda i: (i, cu)),
                  pl.BlockSpec((tm, dc), lambda i: (i, cc)),
                  pl.BlockSpec((tm, dc), lambda i: (i, cb)),
                  pl.BlockSpec((SUBLANES, dc), lambda i: (jnp.maximum(i * hb - 1, 0), cu)),
                  pl.BlockSpec((SUBLANES, dc), lambda i: (jnp.maximum(i * hb - 1, 0), cc)),
                  pl.BlockSpec((3, dc), lambda i: (0, 0))],
        out_specs=(pl.BlockSpec((tm, dc), lambda i: (i, 0)),
                   pl.BlockSpec((2, dc), lambda i: (0, 0))),
        compiler_params=_cparams(1, 32),
        name="conv_prompt",
    )(proj, proj, proj, proj, proj, conv_w)

    rb = t_prompt // s_len
    y_s, st_s = pl.pallas_call(
        _conv_sample_kernel,
        out_shape=(jax.ShapeDtypeStruct((n_seq * s_len, dc), BF16),
                   jax.ShapeDtypeStruct((n_seq, 2, dc), F32)),
        grid=(n_seq,),
        in_specs=[pl.BlockSpec((s_len, dc), lambda b: (rb + b, cu)),
                  pl.BlockSpec((s_len, dc), lambda b: (rb + b, cc)),
                  pl.BlockSpec((s_len, dc), lambda b: (rb + b, cb)),
                  pl.BlockSpec((None, 2, dc), lambda b: (b, 0, 0)),
                  pl.BlockSpec((3, dc), lambda b: (0, 0))],
        out_specs=(pl.BlockSpec((s_len, dc), lambda b: (b, 0)),
                   pl.BlockSpec((None, 2, dc), lambda b: (b, 0, 0))),
        compiler_params=_cparams(1, 32),
        name="conv_sample",
    )(proj, proj, proj, state, conv_w)
    return jnp.concatenate([y_p, y_s], axis=0), st_p, st_s


def _sort_key(score):
    score = jnp.where(score == 0.0, 0.0, score)
    bits = lax.bitcast_convert_type(score, I32)
    return jnp.where(bits < 0, bits ^ 0x7FFFFFFF, bits)


def _kth_largest_key(count_ge, k_top, rows):
    zero = jnp.zeros((rows, 1), I32)
    base = jnp.where(count_ge(zero) >= k_top, zero, jnp.full((rows, 1), INT_MIN, I32))

    def body(it, base):
        cand = base | (jnp.int32(1) << (30 - it))
        return jnp.where(count_ge(cand) >= k_top, cand, base)

    return lax.fori_loop(0, 31, body, base)


def _tie_cut(count_tie_below, need, n_bits, rows):
    def body(it, p):
        cand = p | (jnp.int32(1) << (n_bits - 1 - it))
        return jnp.where(count_tie_below(cand) < need, cand, p)

    return lax.fori_loop(0, n_bits, body, jnp.zeros((rows, 1), I32))


def _select(key, pos, thr, cut):
    return (key > thr) | ((key == thr) & (pos <= cut))


def _index_scores(qi, w, ikc, n_heads, rows):
    s_all = jnp.dot(qi, ikc, preferred_element_type=F32)
    acc = None
    for h in range(n_heads):
        term = w[:, h:h + 1] * jnp.maximum(s_all[h * rows:(h + 1) * rows], 0.0)
        acc = term if acc is None else acc + term
    return acc


def _lane_fold(x, op):
    out = x[:, 0:LANES]
    for j in range(1, x.shape[1] // LANES):
        out = op(out, x[:, j * LANES:(j + 1) * LANES])
    return out


def _softmax_step(lg, madd, bias_slab, v_g, m_ref, l_ref, acc_ref, g):
    rows_g = lg.shape[0]
    dh = acc_ref.shape[-1]

    def logits(r0):
        q_off = r0 % QB
        x = lg[r0:r0 + SLAB] + madd[q_off:q_off + SLAB]
        b = bias_slab(r0 // QB, q_off)
        return x if b is None else x + b

    part = jnp.concatenate([_lane_fold(logits(r0), jnp.maximum) for r0 in range(0, rows_g, SLAB)], axis=0)
    m_old = m_ref[g]
    m_new = jnp.maximum(m_old, jnp.broadcast_to(jnp.max(part, axis=1, keepdims=True), m_old.shape))
    alpha = jnp.exp2(m_old - m_new)
    ps = []
    for r0 in range(0, rows_g, SLAB):
        x = logits(r0)
        m_s = m_new[r0:r0 + SLAB]
        ps.append(jnp.concatenate(
            [jnp.exp2(x[:, j * LANES:(j + 1) * LANES] - m_s) for j in range(x.shape[1] // LANES)],
            axis=1).astype(BF16))
    pv = jnp.dot(jnp.concatenate(ps, axis=0), v_g, preferred_element_type=F32)
    m_ref[g] = m_new
    l_ref[g] = alpha * l_ref[g] + pv[:, dh:]
    acc_ref[g] = alpha * acc_ref[g] + pv[:, :dh]


def _dsa_prompt_kernel(qb_ref, kb_ref, kind_ref, first_ref,
                       q_ref, iq_ref, ikw_ref, ikt_ref, ktf_ref, vf_ref,
                       kta_ref, ktb_ref, va_ref, vb_ref, nb_ref,
                       o_ref,
                       sc_ref, thr_ref, cut_ref, qs_ref, m_ref, l_ref, acc_ref,
                       *, k_top, n_kv, group, dh, idx_dim, n_pos_bits):
    s = pl.program_id(0)
    qb = qb_ref[s]
    kb = kb_ref[s]
    q0 = qb * QB
    coef = (dh ** -0.5) * LOG2E
    row = lax.broadcasted_iota(I32, (QB, 1), 0)
    limit = q0 + (row // CHUNK + 1) * CHUNK

    @pl.when(first_ref[s] == 1)
    def _setup():
        iq = iq_ref[...]
        qi = jnp.concatenate(
            [iq[:, h * idx_dim:(h + 1) * idx_dim] for h in range(N_IDX_HEADS)], axis=0).astype(BF16)
        w = ikw_ref[:, idx_dim:idx_dim + N_IDX_HEADS] * ((N_IDX_HEADS ** -0.5) * (idx_dim ** -0.5))
        n_sw = qb // SUB + 1

        def score_body(c, carry):
            ikc = jnp.concatenate([ikt_ref[c * SUB + j] for j in range(SUB)], axis=1)
            sc = _index_scores(qi, w, ikc, N_IDX_HEADS, QB)
            pos = c * KC + lax.broadcasted_iota(I32, (QB, KC), 1)
            key = _sort_key(jnp.where(pos < limit, sc, -jnp.inf))
            for j in range(SUB):
                sc_ref[c * SUB + j] = key[:, j * QB:(j + 1) * QB]
            return carry

        lax.fori_loop(0, n_sw, score_body, 0)

        def count_ge(cand):
            def body(c, acc):
                for j in range(SUB):
                    acc = acc + (sc_ref[c * SUB + j] >= cand).astype(I32)
                return acc
            acc = lax.fori_loop(0, n_sw, body, jnp.zeros((QB, QB), I32))
            return jnp.sum(acc, axis=1, keepdims=True)

        thr_k = _kth_largest_key(count_ge, k_top, QB)
        thr_ref[...] = thr_k
        cut_ref[...] = jnp.full((QB, 1), INT_MAX, I32)
        excess = count_ge(thr_k) > k_top

        @pl.when(jnp.max(excess.astype(I32)) > 0)
        def _ties():
            def count_tie_below(p):
                def body(c, acc):
                    for j in range(SUB):
                        pos = (c * SUB + j) * QB + lax.broadcasted_iota(I32, (QB, QB), 1)
                        acc = acc + ((sc_ref[c * SUB + j] == thr_k) & (pos < p)).astype(I32)
                    return acc
                acc = lax.fori_loop(0, n_sw, body, jnp.zeros((QB, QB), I32))
                return jnp.sum(acc, axis=1, keepdims=True)

            need = k_top - count_ge(thr_k + 1)
            cut = _tie_cut(count_tie_below, need, n_pos_bits, QB)
            cut_ref[...] = jnp.where(excess, cut, INT_MAX)

        q = q_ref[...] * coef
        for g in range(n_kv):
            qs_ref[g] = jnp.concatenate(
                [q[:, (g * group + r) * dh:(g * group + r + 1) * dh] for r in range(group)],
                axis=0).astype(BF16)
        m_ref[...] = jnp.full(m_ref.shape, NEG, F32)
        l_ref[...] = jnp.zeros(l_ref.shape, F32)
        acc_ref[...] = jnp.zeros(acc_ref.shape, F32)

    thr = thr_ref[...]
    cut = cut_ref[...]

    @pl.when(kind_ref[s] == 0)
    def _far():
        pos = kb * KC + lax.broadcasted_iota(I32, (QB, KC), 1)
        key = jnp.concatenate([sc_ref[kb * SUB + j] for j in range(SUB)], axis=1)
        sel = _select(key, pos, thr, cut)
        sel = sel & (pos < q0 - QB)
        madd = jnp.where(sel, 0.0, NEG)
        lgs = [jnp.dot(qs_ref[g], ktf_ref[g * dh:(g + 1) * dh, :], preferred_element_type=F32)
               for g in range(n_kv)]
        for g in range(n_kv):
            _softmax_step(lgs[g], madd, lambda hh, q_off: None,
                          vf_ref[:, g * 2 * dh:(g + 1) * 2 * dh], m_ref, l_ref, acc_ref, g)

    @pl.when(kind_ref[s] == 1)
    def _window():
        w_keys = 2 * QB
        pos = q0 - QB + lax.broadcasted_iota(I32, (QB, w_keys), 1)
        key = jnp.concatenate([sc_ref[jnp.maximum(qb - 1, 0)], sc_ref[qb]], axis=1)
        sel = _select(key, pos, thr, cut) & (pos >= 0) & (pos < limit)
        kt = jnp.concatenate([kta_ref[...], ktb_ref[...]], axis=1)
        v = jnp.concatenate([va_ref[...], vb_ref[...]], axis=0)
        madd = jnp.where(sel, 0.0, NEG)
        outs = []
        for g in range(n_kv):
            lg = jnp.dot(qs_ref[g], kt[g * dh:(g + 1) * dh, :], preferred_element_type=F32)
            _softmax_step(lg, madd,
                          lambda hh, q_off, g=g: nb_ref[g * group + hh, q_off:q_off + SLAB, :],
                          v[:, g * 2 * dh:(g + 1) * 2 * dh], m_ref, l_ref, acc_ref, g)
            out_g = acc_ref[g] / l_ref[g]
            outs += [out_g[r * QB:(r + 1) * QB] for r in range(group)]
        o_ref[...] = jnp.concatenate(outs, axis=1).astype(o_ref.dtype)


def _t5_bucket(rel, n_buckets):
    nb = n_buckets // 2
    max_exact = nb // 2
    n = jnp.abs(rel)
    nf = jnp.maximum(n, 1).astype(F32)
    large = max_exact + (jnp.log(nf / max_exact) / math.log(MAX_DISTANCE / max_exact)
                         * (nb - max_exact)).astype(I32)
    large = jnp.minimum(large, nb - 1)
    return jnp.where(rel > 0, nb, 0) + jnp.where(n < max_exact, n, large)


def _dsa_prompt(proj, lay, ik_t3, k_t, v_bf, rel_bias, t, n_kv, dh, d_attn, idx_dim):
    n_heads = d_attn // dh
    group = n_heads // n_kv
    d_kv = n_kv * dh
    nqb = t // QB
    k_top = min(TOPK_MAX, t // 4)
    n_buckets = rel_bias.shape[0]
    assert dh == LANES, "row statistics are kept replicated across one lane tile"
    v_bf = jnp.concatenate([v_bf.reshape(t, n_kv, dh), jnp.ones((t, n_kv, dh), v_bf.dtype)],
                           axis=-1).reshape(t, 2 * d_kv)

    rel = (jnp.arange(2 * QB, dtype=I32)[None, :] - QB) - jnp.arange(QB, dtype=I32)[:, None]
    nb = rel_bias[_t5_bucket(rel, n_buckets)].astype(F32)
    nb = jnp.moveaxis(nb, -1, 0) - rel_bias[n_buckets // 2 - 1].astype(F32)[:, None, None]
    nb = nb * LOG2E

    s_qb, s_kb, s_kind, s_first = [], [], [], []
    for qb in range(nqb):
        n_far = -(-max(qb - 1, 0) * QB // KC)
        for kb in range(n_far):
            s_qb.append(qb); s_kb.append(kb); s_kind.append(0); s_first.append(int(kb == 0))
        s_qb.append(qb); s_kb.append(max(n_far - 1, 0)); s_kind.append(1); s_first.append(int(n_far == 0))
    tabs = [jnp.asarray(np.asarray(a, np.int32)) for a in (s_qb, s_kb, s_kind, s_first)]
    n_steps = len(s_qb)

    cq = lay["q"] // d_attn
    ciq = lay["iq"] // (N_IDX_HEADS * idx_dim)
    cikw = lay["ikw"] // LANES
    prev = lambda s, qb, kb, kd, fs: jnp.maximum(qb[s] - 1, 0)

    grid_spec = pltpu.PrefetchScalarGridSpec(
        num_scalar_prefetch=4,
        grid=(n_steps,),
        in_specs=[
            pl.BlockSpec((QB, d_attn), lambda s, qb, kb, kd, fs: (qb[s], cq)),
            pl.BlockSpec((QB, N_IDX_HEADS * idx_dim), lambda s, qb, kb, kd, fs: (qb[s], ciq)),
            pl.BlockSpec((QB, LANES), lambda s, qb, kb, kd, fs: (qb[s], cikw)),
            pl.BlockSpec((nqb, idx_dim, QB), lambda s, qb, kb, kd, fs: (0, 0, 0)),
            pl.BlockSpec((d_kv, KC), lambda s, qb, kb, kd, fs: (0, kb[s])),
            pl.BlockSpec((KC, 2 * d_kv), lambda s, qb, kb, kd, fs: (kb[s], 0)),
            pl.BlockSpec((d_kv, QB), lambda s, qb, kb, kd, fs: (0, prev(s, qb, kb, kd, fs))),
            pl.BlockSpec((d_kv, QB), lambda s, qb, kb, kd, fs: (0, qb[s])),
            pl.BlockSpec((QB, 2 * d_kv), lambda s, qb, kb, kd, fs: (prev(s, qb, kb, kd, fs), 0)),
            pl.BlockSpec((QB, 2 * d_kv), lambda s, qb, kb, kd, fs: (qb[s], 0)),
            pl.BlockSpec((n_heads, QB, 2 * QB), lambda s, qb, kb, kd, fs: (0, 0, 0)),
        ],
        out_specs=pl.BlockSpec((QB, d_attn), lambda s, qb, kb, kd, fs: (qb[s], 0)),
        scratch_shapes=[
            pltpu.VMEM((nqb, QB, QB), I32),
            pltpu.VMEM((QB, 1), I32),
            pltpu.VMEM((QB, 1), I32),
            pltpu.VMEM((n_kv, group * QB, dh), BF16),
            pltpu.VMEM((n_kv, group * QB, LANES), F32),
            pltpu.VMEM((n_kv, group * QB, LANES), F32),
            pltpu.VMEM((n_kv, group * QB, dh), F32),
        ],
    )
    kern = functools.partial(_dsa_prompt_kernel, k_top=k_top, n_kv=n_kv, group=group, dh=dh,
                             idx_dim=idx_dim, n_pos_bits=(t - 1).bit_length())
    return pl.pallas_call(
        kern,
        out_shape=jax.ShapeDtypeStruct((t, d_attn), BF16),
        grid_spec=grid_spec,
        compiler_params=_cparams(1, 48),
        name="dsa_prompt",
    )(*tabs, proj, proj, proj, ik_t3, k_t, v_bf, k_t, k_t, v_bf, v_bf, nb)


def _dsa_sample_kernel(q_ref, iq_ref, ikw_ref, ikt_ref, kt_ref, v_ref, bias_ref, o_ref,
                       *, k_top, n_kv, group, dh, idx_dim, past, n_keys):
    s_len = q_ref.shape[0]
    lp = kt_ref.shape[1]
    scale = dh ** -0.5
    iq = iq_ref[...]
    qi = jnp.concatenate(
        [iq[:, h * idx_dim:(h + 1) * idx_dim] for h in range(N_IDX_HEADS)], axis=0).astype(BF16)
    w = ikw_ref[:, idx_dim:idx_dim + N_IDX_HEADS] * ((N_IDX_HEADS ** -0.5) * (idx_dim ** -0.5))
    sc = _index_scores(qi, w, ikt_ref[...], N_IDX_HEADS, s_len)
    row = lax.broadcasted_iota(I32, (s_len, 1), 0)
    limit = jnp.minimum(((past + row) // CHUNK + 1) * CHUNK, n_keys)
    pos = lax.broadcasted_iota(I32, (s_len, lp), 1)
    allowed = pos < limit
    key = _sort_key(jnp.where(allowed, sc, -jnp.inf))

    def count_ge(cand):
        return jnp.sum((key >= cand).astype(I32), axis=1, keepdims=True)

    thr = _kth_largest_key(count_ge, k_top, s_len)

    def count_tie_below(p):
        return jnp.sum(((key == thr) & (pos < p)).astype(I32), axis=1, keepdims=True)

    cut = _tie_cut(count_tie_below, k_top - count_ge(thr + 1), (lp - 1).bit_length(), s_len)
    sel = _select(key, pos, thr, cut) & allowed
    q = q_ref[...]
    outs = []
    for g in range(n_kv):
        qg = jnp.concatenate(
            [q[:, (g * group + r) * dh:(g * group + r + 1) * dh] for r in range(group)],
            axis=0).astype(BF16)
        lg = jnp.dot(qg, kt_ref[g * dh:(g + 1) * dh, :], preferred_element_type=F32) * scale
        lg = lg.reshape(group, s_len, lp) + bias_ref[g * group:(g + 1) * group]
        lm = jnp.where(sel[None], lg, NEG).reshape(group * s_len, lp)
        p = jnp.exp(lm - jnp.max(lm, axis=1, keepdims=True))
        den = jnp.sum(p, axis=1, keepdims=True)
        out_g = jnp.dot(p.astype(BF16), v_ref[:, g * dh:(g + 1) * dh], preferred_element_type=F32) / den
        outs += [out_g[r * s_len:(r + 1) * s_len] for r in range(group)]
    o_ref[...] = jnp.concatenate(outs, axis=1).astype(o_ref.dtype)


def _dsa_sample(proj, lay, ik_t, k_t, v_bf, rel_bias, row0, n_seq, s_len, past, n_kv, dh, d_attn,
                idx_dim):
    n_heads = d_attn // dh
    group = n_heads // n_kv
    d_kv = n_kv * dh
    lp = k_t.shape[2]
    n_keys = past + s_len
    k_top = min(TOPK_MAX, n_keys // 4)
    rel = jnp.arange(lp, dtype=I32)[None, :] - (past + jnp.arange(s_len, dtype=I32))[:, None]
    bias = jnp.moveaxis(rel_bias[_t5_bucket(rel, rel_bias.shape[0])].astype(F32), -1, 0)
    rb = row0 // s_len
    cq = lay["q"] // d_attn
    ciq = lay["iq"] // (N_IDX_HEADS * idx_dim)
    cikw = lay["ikw"] // LANES
    kern = functools.partial(_dsa_sample_kernel, k_top=k_top, n_kv=n_kv, group=group, dh=dh,
                             idx_dim=idx_dim, past=past, n_keys=n_keys)
    return pl.pallas_call(
        kern,
        out_shape=jax.ShapeDtypeStruct((n_seq * s_len, d_attn), BF16),
        grid=(n_seq,),
        in_specs=[
            pl.BlockSpec((s_len, d_attn), lambda b: (rb + b, cq)),
            pl.BlockSpec((s_len, N_IDX_HEADS * idx_dim), lambda b: (rb + b, ciq)),
            pl.BlockSpec((s_len, LANES), lambda b: (rb + b, cikw)),
            pl.BlockSpec((None, idx_dim, lp), lambda b: (b, 0, 0)),
            pl.BlockSpec((None, d_kv, lp), lambda b: (b, 0, 0)),
            pl.BlockSpec((None, lp, d_kv), lambda b: (b, 0, 0)),
            pl.BlockSpec((n_heads, s_len, lp), lambda b: (0, 0, 0)),
        ],
        out_specs=pl.BlockSpec((s_len, d_attn), lambda b: (b, 0)),
        compiler_params=_cparams(1, 32),
        name="dsa_sample",
    )(proj, proj, proj, ik_t, k_t, v_bf, bias)


def _merge_kernel(yc_ref, ya_ref, gc_ref, ga_ref, wc_ref, wa_ref, o_ref, wcb_ref, wab_ref):
    @pl.when(pl.program_id(1) == 0)
    def _():
        wcb_ref[...] = wc_ref[...].astype(BF16)
        wab_ref[...] = wa_ref[...].astype(BF16)

    bc = jnp.dot(yc_ref[...], wcb_ref[...], preferred_element_type=F32)
    ba = jnp.dot(ya_ref[...], wab_ref[...], preferred_element_type=F32)
    merged = jax.nn.sigmoid(gc_ref[...]) * bc + jax.nn.sigmoid(ga_ref[...]) * ba
    o_ref[...] = merged.astype(o_ref.dtype)


def _merge(y_conv, y_attn, proj, lay, w_bc, w_ba):
    m, dc = y_conv.shape
    da = y_attn.shape[1]
    d = w_bc.shape[1]
    tm = _pick_tile(m, 640, 16)
    tn = _pick_tile(d, 512, 128)
    cgc, cga = lay["gc"] // tn, lay["ga"] // tn
    return pl.pallas_call(
        _merge_kernel,
        out_shape=jax.ShapeDtypeStruct((m, d), BF16),
        grid=(d // tn, m // tm),
        in_specs=[pl.BlockSpec((tm, dc), lambda j, i: (i, 0)),
                  pl.BlockSpec((tm, da), lambda j, i: (i, 0)),
                  pl.BlockSpec((tm, tn), lambda j, i: (i, cgc + j)),
                  pl.BlockSpec((tm, tn), lambda j, i: (i, cga + j)),
                  pl.BlockSpec((dc, tn), lambda j, i: (0, j)),
                  pl.BlockSpec((da, tn), lambda j, i: (0, j))],
        out_specs=pl.BlockSpec((tm, tn), lambda j, i: (i, j)),
        scratch_shapes=[pltpu.VMEM((dc, tn), BF16), pltpu.VMEM((da, tn), BF16)],
        compiler_params=_cparams(2, 48),
        name="merge",
    )(y_conv, y_attn, proj, proj, w_bc, w_ba)


def _out_kernel(a_ref, x_ref, w_ref, o_ref, wb_ref):
    @pl.when(pl.program_id(1) == 0)
    def _():
        wb_ref[...] = w_ref[...].astype(BF16)

    o_ref[...] = x_ref[...] + jnp.dot(a_ref[...], wb_ref[...], preferred_element_type=F32)


def _out_proj(merged, x, w_out):
    m, k = merged.shape
    n = w_out.shape[1]
    tm = _pick_tile(m, 640, 16)
    tn = _pick_tile(n, 512, 128)
    return pl.pallas_call(
        _out_kernel,
        out_shape=jax.ShapeDtypeStruct((m, n), F32),
        grid=(n // tn, m // tm),
        in_specs=[pl.BlockSpec((tm, k), lambda j, i: (i, 0)),
                  pl.BlockSpec((tm, tn), lambda j, i: (i, j)),
                  pl.BlockSpec((k, tn), lambda j, i: (0, j))],
        out_specs=pl.BlockSpec((tm, tn), lambda j, i: (i, j)),
        scratch_shapes=[pltpu.VMEM((k, tn), BF16)],
        compiler_params=_cparams(2, 48),
        name="out_proj",
    )(merged, x, w_out)


def _split_bf16(x):
    hi = x.astype(BF16)
    lo = (x - hi.astype(F32)).astype(BF16)
    return hi, lo


def _router_kernel(x_ref, g_ref, rw_ref, rb_ref, h_ref, idx_ref, gate_ref, rank_ref, cnt_ref, carry_ref):
    @pl.when(pl.program_id(0) == 0)
    def _():
        carry_ref[...] = jnp.zeros(carry_ref.shape, F32)

    xf = x_ref[...]
    xf = xf * lax.rsqrt(jnp.mean(xf * xf, axis=-1, keepdims=True) + EPS)
    h = xf * g_ref[...]
    h_ref[...] = h
    h_hi, h_lo = _split_bf16(h)
    w_hi, w_lo = _split_bf16(rw_ref[...])
    logits = (jnp.dot(h_hi, w_hi, preferred_element_type=F32)
              + (jnp.dot(h_hi, w_lo, preferred_element_type=F32)
                 + jnp.dot(h_lo, w_hi, preferred_element_type=F32)))
    logits = logits + rb_ref[...]
    n_e = logits.shape[1]
    lane = lax.broadcasted_iota(I32, logits.shape, 1)
    vals, idxs = [], []
    cur = logits
    for _ in range(EXPERT_TOP_K):
        mx = jnp.max(cur, axis=1, keepdims=True)
        ix = jnp.min(jnp.where(cur == mx, lane, n_e), axis=1, keepdims=True)
        vals.append(mx)
        idxs.append(ix)
        cur = jnp.where(lane == ix, -jnp.inf, cur)
    top = jnp.concatenate(vals, axis=1)
    e = jnp.exp(top - vals[0])
    gate_ref[...] = e / jnp.sum(e, axis=1, keepdims=True)
    idx_ref[...] = jnp.concatenate(idxs, axis=1)

    tm = logits.shape[0]
    hot = [(lane == ix).astype(F32) for ix in idxs]
    total = hot[0]
    for oh in hot[1:]:
        total = total + oh
    ri = lax.broadcasted_iota(I32, (tm, tm), 0)
    ci = lax.broadcasted_iota(I32, (tm, tm), 1)
    tri = (ci < ri).astype(BF16)
    before = jnp.dot(tri, total.astype(BF16), preferred_element_type=F32) + carry_ref[...]
    rank_ref[...] = jnp.concatenate(
        [jnp.sum(oh * before, axis=1, keepdims=True) for oh in hot], axis=1).astype(I32)
    carry_ref[...] = carry_ref[...] + jnp.sum(total, axis=0, keepdims=True)
    cnt_ref[...] = carry_ref[...].astype(I32)


def _router(x1, g_ffn, router_w, router_b):
    m, d = x1.shape
    n_e = router_w.shape[1]
    tm = _pick_tile(m, 256, SUBLANES)
    row_spec = pl.BlockSpec((tm, EXPERT_TOP_K), lambda i: (i, 0))
    return pl.pallas_call(
        _router_kernel,
        out_shape=(jax.ShapeDtypeStruct((m, d), F32),
                   jax.ShapeDtypeStruct((m, EXPERT_TOP_K), I32),
                   jax.ShapeDtypeStruct((m, EXPERT_TOP_K), F32),
                   jax.ShapeDtypeStruct((m, EXPERT_TOP_K), I32),
                   jax.ShapeDtypeStruct((1, n_e), I32)),
        grid=(m // tm,),
        in_specs=[pl.BlockSpec((tm, d), lambda i: (i, 0)),
                  pl.BlockSpec((1, d), lambda i: (0, 0)),
                  pl.BlockSpec((d, n_e), lambda i: (0, 0)),
                  pl.BlockSpec((1, n_e), lambda i: (0, 0))],
        out_specs=(pl.BlockSpec((tm, d), lambda i: (i, 0)), row_spec, row_spec, row_spec,
                   pl.BlockSpec((1, n_e), lambda i: (0, 0))),
        scratch_shapes=[pltpu.VMEM((1, n_e), F32)],
        compiler_params=_cparams(1, 48),
        name="router",
    )(x1, g_ffn.reshape(1, d), router_w, router_b.reshape(1, n_e))


def _row_copy(src_hbm, buf, sem, src_row, dst_row):
    return pltpu.make_async_copy(src_hbm.at[src_row], buf.at[dst_row], sem)


RUN_BLOCKS = 5
GATHER_ROWS = 128
F_VALID, F_LOAD, F_WCHG = 1, 2, 4


def _up_kernel(se_ref, sf_ref, sb_ref, ss_ref, fl_ref, so_ref,
               tok_hbm, h_hbm, wg_ref, wl_ref, bg_ref, bl_ref, o_ref,
               xs_ref, stage_ref, tok_smem, wgb_ref, wlb_ref, sem_idx, sem):
    s = pl.program_id(0)
    flag = fl_ref[s]
    slot = ss_ref[s]
    tm = xs_ref.shape[1]
    n_burst = tm // GATHER_ROWS

    @pl.when((flag & F_LOAD) != 0)
    def _gather():
        cp = pltpu.make_async_copy(tok_hbm.at[sb_ref[s]], tok_smem, sem_idx)
        cp.start()
        cp.wait()

        def issue(c):
            def body(r, carry):
                _row_copy(h_hbm, stage_ref.at[c % 2], sem.at[c % 2],
                          tok_smem[c * GATHER_ROWS + r], r).start()
                return carry
            lax.fori_loop(0, GATHER_ROWS, body, 0, unroll=8)

        def drain(c):
            def body(r, carry):
                _row_copy(h_hbm, stage_ref.at[c % 2], sem.at[c % 2], 0, r).wait()
                return carry
            lax.fori_loop(0, GATHER_ROWS, body, 0, unroll=8)

        issue(0)
        for c in range(n_burst):
            if c + 1 < n_burst:
                issue(c + 1)
            drain(c)
            xs_ref[slot, c * GATHER_ROWS:(c + 1) * GATHER_ROWS, :] = stage_ref[c % 2].astype(BF16)

    @pl.when((flag & F_WCHG) != 0)
    def _cast():
        wgb_ref[...] = wg_ref[...].astype(BF16)
        wlb_ref[...] = wl_ref[...].astype(BF16)

    @pl.when((flag & F_VALID) != 0)
    def _compute():
        x = xs_ref[slot]
        hg = jnp.dot(x, wgb_ref[...], preferred_element_type=F32) + bg_ref[...]
        hl = jnp.dot(x, wlb_ref[...], preferred_element_type=F32) + bl_ref[...]
        x_glu = jnp.minimum(hg, SWIGLU_LIMIT)
        x_lin = jnp.clip(hl, -SWIGLU_LIMIT, SWIGLU_LIMIT)
        act = x_glu * jax.nn.sigmoid(SWIGLU_ALPHA * x_glu) * (x_lin + 1.0)
        o_ref[...] = act.astype(o_ref.dtype)

    @pl.when((flag & F_VALID) == 0)
    def _unused():
        o_ref[...] = jnp.zeros(o_ref.shape, o_ref.dtype)


def _expert_up(h2, row_tok, tabs, w_up, b_up, tm, tf):
    d = h2.shape[1]
    n_e, _, f2 = w_up.shape
    f = f2 // 2
    nf = f // tf
    n_rows = row_tok.shape[0]
    n_steps = tabs[0].shape[0]
    grid_spec = pltpu.PrefetchScalarGridSpec(
        num_scalar_prefetch=6,
        grid=(n_steps,),
        in_specs=[
            pl.BlockSpec(memory_space=pl.ANY),
            pl.BlockSpec(memory_space=pl.ANY),
            pl.BlockSpec((None, d, tf), lambda s, se, sf, sb, ss, fl, so:(se[s], 0, sf[s])),
            pl.BlockSpec((None, d, tf), lambda s, se, sf, sb, ss, fl, so:(se[s], 0, nf + sf[s])),
            pl.BlockSpec((None, 1, tf), lambda s, se, sf, sb, ss, fl, so:(se[s], 0, sf[s])),
            pl.BlockSpec((None, 1, tf), lambda s, se, sf, sb, ss, fl, so:(se[s], 0, nf + sf[s])),
        ],
        out_specs=pl.BlockSpec((tm, tf), lambda s, se, sf, sb, ss, fl, so: (sb[s], so[s])),
        scratch_shapes=[pltpu.VMEM((RUN_BLOCKS, tm, d), BF16),
                        pltpu.VMEM((2, GATHER_ROWS, d), h2.dtype),
                        pltpu.SMEM((tm,), I32),
                        pltpu.VMEM((d, tf), BF16), pltpu.VMEM((d, tf), BF16),
                        pltpu.SemaphoreType.DMA(()),
                        pltpu.SemaphoreType.DMA((2,))],
    )
    return pl.pallas_call(
        _up_kernel,
        out_shape=jax.ShapeDtypeStruct((n_rows, f), BF16),
        grid_spec=grid_spec,
        compiler_params=_cparams(1, 56),
        name="moe_up",
    )(*tabs, row_tok.reshape(n_rows // tm, tm), h2, w_up, w_up,
      b_up.reshape(n_e, 1, f2), b_up.reshape(n_e, 1, f2))


def _down_kernel(se_ref, sf_ref, sb_ref, ss_ref, fl_ref, so_ref,
                 a_hbm, w_ref, b_ref, o_ref, as_ref, wb_ref, sem):
    s = pl.program_id(0)
    flag = fl_ref[s]
    slot = ss_ref[s]
    tm = as_ref.shape[1]

    @pl.when((flag & F_LOAD) != 0)
    def _load():
        row0 = pl.multiple_of(sb_ref[s] * tm, tm)
        cp = pltpu.make_async_copy(a_hbm.at[pl.ds(row0, tm)], as_ref.at[slot], sem)
        cp.start()
        cp.wait()

    @pl.when((flag & F_WCHG) != 0)
    def _cast():
        wb_ref[...] = w_ref[...].astype(BF16)

    @pl.when((flag & F_VALID) != 0)
    def _compute():
        o_ref[...] = jnp.dot(as_ref[slot], wb_ref[...], preferred_element_type=F32) + b_ref[...]

    @pl.when((flag & F_VALID) == 0)
    def _unused():
        o_ref[...] = jnp.zeros(o_ref.shape, o_ref.dtype)


def _expert_down(act, tabs, w_down, b_down, tm, tn):
    n_rows, f = act.shape
    n_e, _, d = w_down.shape
    n_steps = tabs[0].shape[0]
    grid_spec = pltpu.PrefetchScalarGridSpec(
        num_scalar_prefetch=6,
        grid=(n_steps,),
        in_specs=[
            pl.BlockSpec(memory_space=pl.ANY),
            pl.BlockSpec((None, f, tn), lambda s, se, sf, sb, ss, fl, so:(se[s], 0, sf[s])),
            pl.BlockSpec((None, 1, tn), lambda s, se, sf, sb, ss, fl, so:(se[s], 0, sf[s])),
        ],
        out_specs=pl.BlockSpec((tm, tn), lambda s, se, sf, sb, ss, fl, so: (sb[s], so[s])),
        scratch_shapes=[pltpu.VMEM((RUN_BLOCKS, tm, f), BF16),
                        pltpu.VMEM((f, tn), BF16),
                        pltpu.SemaphoreType.DMA(())],
    )
    return pl.pallas_call(
        _down_kernel,
        out_shape=jax.ShapeDtypeStruct((n_rows, d), F32),
        grid_spec=grid_spec,
        compiler_params=_cparams(1, 56),
        name="moe_down",
    )(*tabs, act, w_down, b_down.reshape(n_e, 1, d))


def _combine_kernel(pos_hbm, y_hbm, x_ref, gt_ref, g_ref, o_ref, pos_smem, buf, sem_idx, sem):
    i = pl.program_id(0)
    n_steps = pl.num_programs(0)
    tb = x_ref.shape[0]
    n = pos_smem.shape[1]
    cur = i % 2
    nxt = 1 - cur

    def idx_copy(step, sl):
        return pltpu.make_async_copy(pos_hbm.at[step], pos_smem.at[sl], sem_idx.at[sl])

    def issue_rows(sl):
        def body(r, c):
            _row_copy(y_hbm, buf.at[sl], sem.at[sl], pos_smem[sl, r], r).start()
            return c
        lax.fori_loop(0, n, body, 0, unroll=8)

    @pl.when(i == 0)
    def _():
        idx_copy(0, 0).start()
        idx_copy(0, 0).wait()
        issue_rows(0)

        @pl.when(n_steps > 1)
        def _():
            idx_copy(1, 1).start()

    @pl.when(i + 1 < n_steps)
    def _():
        idx_copy(i + 1, nxt).wait()
        issue_rows(nxt)

    @pl.when(i + 2 < n_steps)
    def _():
        idx_copy(i + 2, cur).start()

    def drain(r, c):
        _row_copy(y_hbm, buf.at[cur], sem.at[cur], 0, r).wait()
        return c

    lax.fori_loop(0, n, drain, 0, unroll=8)
    gt = gt_ref[...]
    moe = None
    for j in range(EXPERT_TOP_K):
        term = buf[cur, j * tb:(j + 1) * tb, :] * gt[:, j:j + 1]
        moe = term if moe is None else moe + term
    xf = x_ref[...] + moe
    xf = xf * lax.rsqrt(jnp.mean(xf * xf, axis=-1, keepdims=True) + EPS)
    o_ref[...] = xf * g_ref[...]


def _combine(y_rows, pos, gates, x1, g_final, tb):
    m, d = x1.shape
    nb = m // tb
    n = EXPERT_TOP_K * tb
    pos_blocks = pos.reshape(nb, tb, EXPERT_TOP_K).transpose(0, 2, 1).reshape(nb, n)
    return pl.pallas_call(
        _combine_kernel,
        out_shape=jax.ShapeDtypeStruct((m, d), F32),
        grid=(nb,),
        in_specs=[pl.BlockSpec(memory_space=pl.ANY), pl.BlockSpec(memory_space=pl.ANY),
                  pl.BlockSpec((tb, d), lambda i: (i, 0)),
                  pl.BlockSpec((tb, EXPERT_TOP_K), lambda i: (i, 0)),
                  pl.BlockSpec((1, d), lambda i: (0, 0))],
        out_specs=pl.BlockSpec((tb, d), lambda i: (i, 0)),
        scratch_shapes=[pltpu.SMEM((2, n), I32),
                        pltpu.VMEM((2, n, d), F32),
                        pltpu.SemaphoreType.DMA((2,)),
                        pltpu.SemaphoreType.DMA((2,))],
        compiler_params=_cparams(1, 32),
        name="moe_combine",
    )(pos_blocks, y_rows, x1, gates, g_final.reshape(1, d))


def _routing_tables(top_idx, rank, counts, tm):
    t = top_idx.shape[0]
    n = t * EXPERT_TOP_K
    n_experts = counts.shape[0]
    n_blocks = -(-n // tm) + n_experts
    nblk_e = (counts + tm - 1) // tm
    blk_end = jnp.cumsum(nblk_e)
    blk_start = blk_end - nblk_e
    pos = (blk_start * tm)[top_idx] + rank
    tok = jnp.broadcast_to(jnp.arange(t, dtype=I32)[:, None], pos.shape)
    row_tok = jnp.zeros((n_blocks * tm,), I32).at[pos.reshape(n)].set(tok.reshape(n))
    b = jnp.arange(n_blocks, dtype=I32)
    e_b = jnp.minimum(jnp.searchsorted(blk_end, b, side="right"), n_experts - 1).astype(I32)
    in_e = b - blk_start[e_b]
    slot = in_e % RUN_BLOCKS
    run_first = b - slot
    run_len = jnp.minimum(RUN_BLOCKS, nblk_e[e_b] - (in_e - slot))
    n_valid = blk_end[-1]
    return pos, row_tok, (b, e_b, slot, run_first, run_len, n_valid)


def _step_tables(blocks, n_tiles):
    b, e_b, slot, run_first, run_len, n_valid = blocks
    n_blocks = b.shape[0]
    f = jnp.arange(n_tiles, dtype=I32)[None, :]
    valid = (b < n_valid)[:, None]
    sid = n_tiles * run_first[:, None] + f * run_len[:, None] + slot[:, None]
    sid = jnp.where(valid, sid, n_tiles * b[:, None] + f)
    last = jnp.maximum(n_valid - 1, 0)
    full = lambda a: jnp.broadcast_to(a, (n_blocks, n_tiles))
    vals = jnp.stack([
        jnp.where(valid, full(e_b[:, None]), e_b[last]),
        jnp.where(valid, full(f), n_tiles - 1),
        full(b[:, None]),
        jnp.where(valid, full(slot[:, None]), slot[last]),
        jnp.where(valid, F_VALID + jnp.where(f == 0, F_LOAD, 0), 0),
        full(f),
    ], axis=-1).reshape(n_blocks * n_tiles, 6).astype(I32)
    tab = jnp.zeros((n_blocks * n_tiles, 6), I32).at[sid.reshape(-1)].set(vals)
    se, sf, sb, ss, fl, so = (tab[:, k] for k in range(6))
    prev = lambda a: jnp.concatenate([a[:1] - 1, a[:-1]])
    wchg = (se != prev(se)) | (sf != prev(sf))
    fl = fl + jnp.where(wchg & ((fl & F_VALID) != 0), F_WCHG, 0)
    return se, sf, sb, ss, fl, so


def _proj_layout(d, dc, d_attn, d_kv, idx_dim):
    iq_w = N_IDX_HEADS * idx_dim
    assert idx_dim + N_IDX_HEADS <= LANES
    segs = [("gc", d), ("ga", d), ("u", dc), ("c", dc), ("b", dc), ("q", d_attn),
            ("iq", iq_w), ("k", d_kv), ("v", d_kv), ("ikw", LANES)]
    segs.sort(key=lambda s: -s[1])
    lay, off = {}, 0
    for name, width in segs:
        assert off % width == 0, (name, off, width)
        lay[name] = off
        off += width
    lay["total"] = -(-off // 512) * 512
    return lay


def _relayout_w_in(w, lay, dc, d_attn, d_kv, idx_dim, d):
    widths = (dc, dc, dc, d_attn, d_kv, d_kv, N_IDX_HEADS * idx_dim, idx_dim, N_IDX_HEADS, d, d)
    names = ("u", "c", "b", "q", "k", "v", "iq", "ik", "iw", "gc", "ga")
    src, o = {}, 0
    for nme, wd in zip(names, widths):
        src[nme] = (o, wd)
        o += wd
    assert o == w.shape[1]
    out = jnp.zeros((w.shape[0], lay["total"]), w.dtype)
    for nme in ("gc", "ga", "u", "c", "b", "q", "iq", "k", "v"):
        s0, wd = src[nme]
        out = lax.dynamic_update_slice(out, w[:, s0:s0 + wd], (0, lay[nme]))
    s0 = src["ik"][0]
    out = lax.dynamic_update_slice(out, w[:, s0:s0 + idx_dim + N_IDX_HEADS], (0, lay["ikw"]))
    return out


def kernel(x_prompt, x_sample, cache_k, cache_v, cache_idx_k, state_conv, g_mix, w_in, conv_w,
           w_branch_conv, w_branch_attn, w_out, rel_bias, g_ffn, router_w, router_b, w_up, b_up,
           w_down, b_down, g_final):
    depth = w_in.shape[0]
    assert depth == 1, "one trunk layer"
    bp, t, d = x_prompt.shape
    assert bp == 1
    n_seq, s_len, _ = x_sample.shape
    past, n_kv, dh = cache_k.shape[2:5]
    idx_dim = cache_idx_k.shape[-1]
    dc = conv_w.shape[-1]
    d_attn = w_branch_attn.shape[1]
    d_kv = n_kv * dh
    n_experts = router_w.shape[-1]
    assert t % KC == 0 and s_len % 16 == 0

    lay = _proj_layout(d, dc, d_attn, d_kv, idx_dim)
    w_in_r = _relayout_w_in(w_in[0], lay, dc, d_attn, d_kv, idx_dim, d)

    x_all = jnp.concatenate([x_prompt.reshape(t, d), x_sample.reshape(n_seq * s_len, d)], axis=0)
    h = _rmsnorm(x_all, g_mix[0], BF16)
    proj = _matmul(h, w_in_r)

    k_new = proj[:, lay["k"]:lay["k"] + d_kv]
    v_new = proj[:, lay["v"]:lay["v"] + d_kv]
    ik_new = proj[:, lay["ikw"]:lay["ikw"] + idx_dim]

    y_conv, st_p, st_s = _short_conv(proj, lay, conv_w[0], state_conv[0], t, n_seq, s_len)

    k_t = k_new[:t].T.astype(BF16)
    v_bf = v_new[:t].astype(BF16)
    ik_t3 = ik_new[:t].reshape(t // QB, QB, idx_dim).transpose(0, 2, 1).astype(BF16)
    ya_p = _dsa_prompt(proj, lay, ik_t3, k_t, v_bf, rel_bias, t, n_kv, dh, d_attn, idx_dim)

    n_keys = past + s_len
    lp = -(-n_keys // LANES) * LANES

    def with_cache(cache, new, width):
        full = jnp.concatenate([cache.reshape(n_seq, past, width),
                                new[t:].reshape(n_seq, s_len, width)], axis=1)
        return jnp.pad(full, ((0, 0), (0, lp - n_keys), (0, 0))).astype(BF16)

    k_s = with_cache(cache_k[0], k_new, d_kv)
    v_s = with_cache(cache_v[0], v_new, d_kv)
    ik_s = with_cache(cache_idx_k[0], ik_new, idx_dim)
    ya_s = _dsa_sample(proj, lay, ik_s.transpose(0, 2, 1), k_s.transpose(0, 2, 1), v_s, rel_bias,
                       t, n_seq, s_len, past, n_kv, dh, d_attn, idx_dim)
    y_attn = jnp.concatenate([ya_p, ya_s], axis=0)

    merged = _merge(y_conv, y_attn, proj, lay, w_branch_conv[0], w_branch_attn[0])
    x1 = _out_proj(merged, x_all, w_out[0])

    h2, top_idx, gates, rank, counts = _router(x1, g_ffn[0], router_w[0], router_b[0])
    tm_e = 512 if (x_all.shape[0] * EXPERT_TOP_K) >= 16 * 512 else 128
    pos, row_tok, blocks = _routing_tables(top_idx, rank, counts.reshape(n_experts), tm_e)
    tf = _pick_tile(w_down.shape[2], 256, LANES)
    tn = _pick_tile(d, 512, LANES)
    act = _expert_up(h2, row_tok, _step_tables(blocks, w_down.shape[2] // tf), w_up[0], b_up[0], tm_e, tf)
    y_rows = _expert_down(act, _step_tables(blocks, d // tn), w_down[0], b_down[0], tm_e, tn)
    tb = _pick_tile(x_all.shape[0], 64, SUBLANES)
    y_all = _combine(y_rows, pos, gates, x1, g_final, tb)

    y_prompt = y_all[:t].reshape(1, t, d)
    y_sample = y_all[t:].reshape(n_seq, s_len, d)
    return (y_prompt, y_sample,
            k_new[:t].reshape(1, 1, t, n_kv, dh), v_new[:t].reshape(1, 1, t, n_kv, dh),
            ik_new[:t].reshape(1, 1, t, idx_dim), st_p.reshape(1, 1, 2, dc),
            k_new[t:].reshape(1, n_seq, s_len, n_kv, dh), v_new[t:].reshape(1, n_seq, s_len, n_kv, dh),
            ik_new[t:].reshape(1, n_seq, s_len, idx_dim), st_s.reshape(1, n_seq, 2, dc))
```

```python
import functools
import math

import numpy as np
import jax
import jax.numpy as jnp
from jax import lax
from jax.experimental import pallas as pl
from jax.experimental.pallas import tpu as pltpu

F32 = jnp.float32
BF16 = jnp.bfloat16
I32 = jnp.int32

CHUNK = 64
EPS = 1e-6
TOPK_MAX = 256
N_IDX_HEADS = 16
MAX_DISTANCE = 128
EXPERT_TOP_K = 4
SWIGLU_LIMIT = 7.0
SWIGLU_ALPHA = 1.702

LANES = 128
SUBLANES = 8
VMEM_BYTES_V7X = 64 * 1024 * 1024

QB = 128
KC = 512
SUB = KC // QB
FKC = 1024
FSUB = FKC // QB
SLAB = 32
LOG2E = 1.4426950408889634
NEG = -1e30
INT_MIN = -(2 ** 31)
INT_MAX = 2 ** 31 - 1


def _cparams(n_axes, vmem_mb):
    return pltpu.CompilerParams(
        dimension_semantics=("arbitrary",) * n_axes,
        vmem_limit_bytes=vmem_mb * 1024 * 1024,
    )


def _pick_tile(n, max_tile, quantum):
    best = None
    t = quantum
    while t <= min(n, max_tile):
        if n % t == 0:
            best = t
        t += quantum
    assert best is not None, (n, max_tile, quantum)
    return best


def _rmsnorm_kernel(x_ref, g_ref, o_ref):
    xf = x_ref[...]
    xf = xf * lax.rsqrt(jnp.mean(xf * xf, axis=-1, keepdims=True) + EPS)
    o_ref[...] = (xf * g_ref[...]).astype(o_ref.dtype)


def _rmsnorm(x, g, out_dtype):
    m, d = x.shape
    tm = _pick_tile(m, 256, SUBLANES * 2)
    return pl.pallas_call(
        _rmsnorm_kernel,
        out_shape=jax.ShapeDtypeStruct((m, d), out_dtype),
        grid=(m // tm,),
        in_specs=[pl.BlockSpec((tm, d), lambda i: (i, 0)),
                  pl.BlockSpec((1, d), lambda i: (0, 0))],
        out_specs=pl.BlockSpec((tm, d), lambda i: (i, 0)),
        compiler_params=_cparams(1, 32),
        name="rmsnorm",
    )(x, g.reshape(1, d))


def _mm_kernel(a_ref, w_ref, o_ref, wb_ref):
    @pl.when(pl.program_id(1) == 0)
    def _():
        wb_ref[...] = w_ref[...].astype(BF16)

    o_ref[...] = jnp.dot(a_ref[...], wb_ref[...], preferred_element_type=F32)


def _matmul(a, w):
    m, k = a.shape
    n = w.shape[1]
    tm = _pick_tile(m, 640, 16)
    tn = _pick_tile(n, 512, 128)
    return pl.pallas_call(
        _mm_kernel,
        out_shape=jax.ShapeDtypeStruct((m, n), F32),
        grid=(n // tn, m // tm),
        in_specs=[pl.BlockSpec((tm, k), lambda j, i: (i, 0)),
                  pl.BlockSpec((k, tn), lambda j, i: (0, j))],
        out_specs=pl.BlockSpec((tm, tn), lambda j, i: (i, j)),
        scratch_shapes=[pltpu.VMEM((k, tn), BF16)],
        compiler_params=_cparams(2, 48),
        name="in_proj",
    )(a, w)


def _conv_prompt_kernel(u_ref, c_ref, b_ref, up_ref, cp_ref, w_ref, y_ref, st_ref):
    i = pl.program_id(0)
    tm = u_ref.shape[0]
    v = c_ref[...] * u_ref[...]
    hv = cp_ref[...] * up_ref[...]
    hv = jnp.where(i > 0, hv, 0.0)
    vp = jnp.concatenate([hv, v], axis=0)
    w = w_ref[...]
    h = SUBLANES
    y = w[0:1] * vp[h - 2:h - 2 + tm] + w[1:2] * vp[h - 1:h - 1 + tm]
    y = y + w[2:3] * v
    y_ref[...] = (b_ref[...] * y).astype(y_ref.dtype)

    @pl.when(i == pl.num_programs(0) - 1)
    def _():
        st_ref[...] = v[tm - 2:tm]


def _conv_sample_kernel(u_ref, c_ref, b_ref, s_ref, w_ref, y_ref, st_ref):
    t = u_ref.shape[0]
    v = c_ref[...] * u_ref[...]
    vp = jnp.concatenate([s_ref[...], v], axis=0)
    w = w_ref[...]
    y = w[0:1] * vp[0:t] + w[1:2] * vp[1:t + 1]
    y = y + w[2:3] * vp[2:t + 2]
    y_ref[...] = (b_ref[...] * y).astype(y_ref.dtype)
    st_ref[...] = vp[t:t + 2]


def _short_conv(proj, lay, conv_w, state, t_prompt, n_seq, s_len):
    dc = conv_w.shape[1]
    cu, cc, cb = lay["u"] // dc, lay["c"] // dc, lay["b"] // dc
    tm = _pick_tile(t_prompt, 256, 16)
    hb = tm // SUBLANES
    y_p, st_p = pl.pallas_call(
        _conv_prompt_kernel,
        out_shape=(jax.ShapeDtypeStruct((t_prompt, dc), BF16),
                   jax.ShapeDtypeStruct((2, dc), F32)),
        grid=(t_prompt // tm,),
        in_specs=[pl.BlockSpec((tm, dc), lambda i: (i, cu)),
                  pl.BlockSpec((tm, dc), lambda i: (i, cc)),
                  pl.BlockSpec((tm, dc), lambda i: (i, cb)),
                  pl.BlockSpec((SUBLANES, dc), lambda i: (jnp.maximum(i * hb - 1, 0), cu)),
                  pl.BlockSpec((SUBLANES, dc), lambda i: (jnp.maximum(i * hb - 1, 0), cc)),
                  pl.BlockSpec((3, dc), lambda i: (0, 0))],
        out_specs=(pl.BlockSpec((tm, dc), lambda i: (i, 0)),
                   pl.BlockSpec((2, dc), lambda i: (0, 0))),
        compiler_params=_cparams(1, 32),
        name="conv_prompt",
    )(proj, proj, proj, proj, proj, conv_w)

    rb = t_prompt // s_len
    y_s, st_s = pl.pallas_call(
        _conv_sample_kernel,
        out_shape=(jax.ShapeDtypeStruct((n_seq * s_len, dc), BF16),
                   jax.ShapeDtypeStruct((n_seq, 2, dc), F32)),
        grid=(n_seq,),
        in_specs=[pl.BlockSpec((s_len, dc), lambda b: (rb + b, cu)),
                  pl.BlockSpec((s_len, dc), lambda b: (rb + b, cc)),
                  pl.BlockSpec((s_len, dc), lambda b: (rb + b, cb)),
                  pl.BlockSpec((None, 2, dc), lambda b: (b, 0, 0)),
                  pl.BlockSpec((3, dc), lambda b: (0, 0))],
        out_specs=(pl.BlockSpec((s_len, dc), lambda b: (b, 0)),
                   pl.BlockSpec((None, 2, dc), lambda b: (b, 0, 0))),
        compiler_params=_cparams(1, 32),
        name="conv_sample",
    )(proj, proj, proj, state, conv_w)
    return jnp.concatenate([y_p, y_s], axis=0), st_p, st_s


def _sort_key(score):
    score = jnp.where(score == 0.0, 0.0, score)
    bits = lax.bitcast_convert_type(score, I32)
    return jnp.where(bits < 0, bits ^ 0x7FFFFFFF, bits)


def _kth_largest_key(count_ge, k_top, rows):
    zero = jnp.zeros((rows, 1), I32)
    base = jnp.where(count_ge(zero) >= k_top, zero, jnp.full((rows, 1), INT_MIN, I32))

    def body(it, base):
        cand = base | (jnp.int32(1) << (30 - it))
        return jnp.where(count_ge(cand) >= k_top, cand, base)

    return lax.fori_loop(0, 31, body, base)


def _tie_cut(count_tie_below, need, n_bits, rows):
    def body(it, p):
        cand = p | (jnp.int32(1) << (n_bits - 1 - it))
        return jnp.where(count_tie_below(cand) < need, cand, p)

    return lax.fori_loop(0, n_bits, body, jnp.zeros((rows, 1), I32))


def _select(key, pos, thr, cut):
    return (key > thr) | ((key == thr) & (pos <= cut))


def _index_scores(qi, w, ikc, n_heads, rows):
    s_all = jnp.dot(qi, ikc, preferred_element_type=F32)
    acc = None
    for h in range(n_heads):
        term = w[:, h:h + 1] * jnp.maximum(s_all[h * rows:(h + 1) * rows], 0.0)
        acc = term if acc is None else acc + term
    return acc


def _lane_fold(x, op):
    out = x[:, 0:LANES]
    for j in range(1, x.shape[1] // LANES):
        out = op(out, x[:, j * LANES:(j + 1) * LANES])
    return out


def _softmax_step(lg, madd, bias_slab, v_g, m_ref, l_ref, acc_ref, g):
    rows_g = lg.shape[0]
    dh = acc_ref.shape[-1]

    def logits(r0):
        q_off = r0 % QB
        x = lg[r0:r0 + SLAB] + madd[q_off:q_off + SLAB]
        b = bias_slab(r0 // QB, q_off)
        return x if b is None else x + b

    part = jnp.concatenate([_lane_fold(logits(r0), jnp.maximum) for r0 in range(0, rows_g, SLAB)], axis=0)
    m_old = m_ref[g]
    m_new = jnp.maximum(m_old, jnp.broadcast_to(jnp.max(part, axis=1, keepdims=True), m_old.shape))
    alpha = jnp.exp2(m_old - m_new)
    ps = []
    for r0 in range(0, rows_g, SLAB):
        x = logits(r0)
        m_s = m_new[r0:r0 + SLAB]
        ps.append(jnp.concatenate(
            [jnp.exp2(x[:, j * LANES:(j + 1) * LANES] - m_s) for j in range(x.shape[1] // LANES)],
            axis=1).astype(BF16))
    pv = jnp.dot(jnp.concatenate(ps, axis=0), v_g, preferred_element_type=F32)
    m_ref[g] = m_new
    l_ref[g] = alpha * l_ref[g] + pv[:, dh:]
    acc_ref[g] = alpha * acc_ref[g] + pv[:, :dh]


def _dsa_prompt_kernel(qb_ref, kb_ref, kind_ref, first_ref,
                       q_ref, iq_ref, ikw_ref, ikt_ref, ktf_ref, vf_ref,
                       kta_ref, ktb_ref, va_ref, vb_ref, nb_ref,
                       o_ref,
                       sc_ref, thr_ref, cut_ref, qs_ref, m_ref, l_ref, acc_ref,
                       *, k_top, n_kv, group, dh, idx_dim, n_pos_bits):
    s = pl.program_id(0)
    qb = qb_ref[s]
    kb = kb_ref[s]
    q0 = qb * QB
    coef = (dh ** -0.5) * LOG2E
    row = lax.broadcasted_iota(I32, (QB, 1), 0)
    limit = q0 + (row // CHUNK + 1) * CHUNK

    @pl.when(first_ref[s] == 1)
    def _setup():
        iq = iq_ref[...]
        qi = jnp.concatenate(
            [iq[:, h * idx_dim:(h + 1) * idx_dim] for h in range(N_IDX_HEADS)], axis=0).astype(BF16)
        w = ikw_ref[:, idx_dim:idx_dim + N_IDX_HEADS] * ((N_IDX_HEADS ** -0.5) * (idx_dim ** -0.5))
        n_sw = qb // SUB + 1

        def score_body(c, carry):
            ikc = jnp.concatenate([ikt_ref[c * SUB + j] for j in range(SUB)], axis=1)
            sc = _index_scores(qi, w, ikc, N_IDX_HEADS, QB)
            pos = c * KC + lax.broadcasted_iota(I32, (QB, KC), 1)
            key = _sort_key(jnp.where(pos < limit, sc, -jnp.inf))
            for j in range(SUB):
                sc_ref[c * SUB + j] = key[:, j * QB:(j + 1) * QB]
            return carry

        lax.fori_loop(0, n_sw, score_body, 0)

        def count_ge(cand):
            def body(c, acc):
                for j in range(SUB):
                    acc = acc + (sc_ref[c * SUB + j] >= cand).astype(I32)
                return acc
            acc = lax.fori_loop(0, n_sw, body, jnp.zeros((QB, QB), I32))
            return jnp.sum(acc, axis=1, keepdims=True)

        thr_k = _kth_largest_key(count_ge, k_top, QB)
        thr_ref[...] = thr_k
        cut_ref[...] = jnp.full((QB, 1), INT_MAX, I32)
        excess = count_ge(thr_k) > k_top

        @pl.when(jnp.max(excess.astype(I32)) > 0)
        def _ties():
            def count_tie_below(p):
                def body(c, acc):
                    for j in range(SUB):
                        pos = (c * SUB + j) * QB + lax.broadcasted_iota(I32, (QB, QB), 1)
                        acc = acc + ((sc_ref[c * SUB + j] == thr_k) & (pos < p)).astype(I32)
                    return acc
                acc = lax.fori_loop(0, n_sw, body, jnp.zeros((QB, QB), I32))
                return jnp.sum(acc, axis=1, keepdims=True)

            need = k_top - count_ge(thr_k + 1)
            cut = _tie_cut(count_tie_below, need, n_pos_bits, QB)
            cut_ref[...] = jnp.where(excess, cut, INT_MAX)

        q = q_ref[...] * coef
        for g in range(n_kv):
            qs_ref[g] = jnp.concatenate(
                [q[:, (g * group + r) * dh:(g * group + r + 1) * dh] for r in range(group)],
                axis=0).astype(BF16)
        m_ref[...] = jnp.full(m_ref.shape, NEG, F32)
        l_ref[...] = jnp.zeros(l_ref.shape, F32)
        acc_ref[...] = jnp.zeros(acc_ref.shape, F32)

    thr = thr_ref[...]
    cut = cut_ref[...]

    @pl.when(kind_ref[s] == 0)
    def _far():
        pos = kb * FKC + lax.broadcasted_iota(I32, (QB, FKC), 1)
        key = jnp.concatenate([sc_ref[jnp.minimum(kb * FSUB + j, qb)] for j in range(FSUB)], axis=1)
        sel = _select(key, pos, thr, cut)
        sel = sel & (pos < q0 - QB)
        madd = jnp.where(sel, 0.0, NEG)
        lgs = [jnp.dot(qs_ref[g], ktf_ref[g * dh:(g + 1) * dh, :], preferred_element_type=F32)
               for g in range(n_kv)]
        for g in range(n_kv):
            _softmax_step(lgs[g], madd, lambda hh, q_off: None,
                          vf_ref[:, g * 2 * dh:(g + 1) * 2 * dh], m_ref, l_ref, acc_ref, g)

    @pl.when(kind_ref[s] == 1)
    def _window():
        w_keys = 2 * QB
        pos = q0 - QB + lax.broadcasted_iota(I32, (QB, w_keys), 1)
        key = jnp.concatenate([sc_ref[jnp.maximum(qb - 1, 0)], sc_ref[qb]], axis=1)
        sel = _select(key, pos, thr, cut) & (pos >= 0) & (pos < limit)
        kt = jnp.concatenate([kta_ref[...], ktb_ref[...]], axis=1)
        v = jnp.concatenate([va_ref[...], vb_ref[...]], axis=0)
        madd = jnp.where(sel, 0.0, NEG)
        outs = []
        for g in range(n_kv):
            lg = jnp.dot(qs_ref[g], kt[g * dh:(g + 1) * dh, :], preferred_element_type=F32)
            _softmax_step(lg, madd,
                          lambda hh, q_off, g=g: nb_ref[g * group + hh, q_off:q_off + SLAB, :],
                          v[:, g * 2 * dh:(g + 1) * 2 * dh], m_ref, l_ref, acc_ref, g)
            out_g = acc_ref[g] / l_ref[g]
            outs += [out_g[r * QB:(r + 1) * QB] for r in range(group)]
        o_ref[...] = jnp.concatenate(outs, axis=1).astype(o_ref.dtype)


def _t5_bucket(rel, n_buckets):
    nb = n_buckets // 2
    max_exact = nb // 2
    n = jnp.abs(rel)
    nf = jnp.maximum(n, 1).astype(F32)
    large = max_exact + (jnp.log(nf / max_exact) / math.log(MAX_DISTANCE / max_exact)
                         * (nb - max_exact)).astype(I32)
    large = jnp.minimum(large, nb - 1)
    return jnp.where(rel > 0, nb, 0) + jnp.where(n < max_exact, n, large)


def _dsa_prompt(proj, lay, ik_t3, k_t, v_bf, rel_bias, t, n_kv, dh, d_attn, idx_dim):
    n_heads = d_attn // dh
    group = n_heads // n_kv
    d_kv = n_kv * dh
    nqb = t // QB
    k_top = min(TOPK_MAX, t // 4)
    n_buckets = rel_bias.shape[0]
    assert dh == LANES, "row statistics are kept replicated across one lane tile"
    v_bf = jnp.concatenate([v_bf.reshape(t, n_kv, dh), jnp.ones((t, n_kv, dh), v_bf.dtype)],
                           axis=-1).reshape(t, 2 * d_kv)

    rel = (jnp.arange(2 * QB, dtype=I32)[None, :] - QB) - jnp.arange(QB, dtype=I32)[:, None]
    nb = rel_bias[_t5_bucket(rel, n_buckets)].astype(F32)
    nb = jnp.moveaxis(nb, -1, 0) - rel_bias[n_buckets // 2 - 1].astype(F32)[:, None, None]
    nb = nb * LOG2E

    s_qb, s_kb, s_kind, s_first = [], [], [], []
    for qb in range(nqb):
        n_far = -(-max(qb - 1, 0) * QB // FKC)
        for kb in range(n_far):
            s_qb.append(qb); s_kb.append(kb); s_kind.append(0); s_first.append(int(kb == 0))
        s_qb.append(qb); s_kb.append(max(n_far - 1, 0)); s_kind.append(1); s_first.append(int(n_far == 0))
    tabs = [jnp.asarray(np.asarray(a, np.int32)) for a in (s_qb, s_kb, s_kind, s_first)]
    n_steps = len(s_qb)

    cq = lay["q"] // d_attn
    ciq = lay["iq"] // (N_IDX_HEADS * idx_dim)
    cikw = lay["ikw"] // LANES
    prev = lambda s, qb, kb, kd, fs: jnp.maximum(qb[s] - 1, 0)

    grid_spec = pltpu.PrefetchScalarGridSpec(
        num_scalar_prefetch=4,
        grid=(n_steps,),
        in_specs=[
            pl.BlockSpec((QB, d_attn), lambda s, qb, kb, kd, fs: (qb[s], cq)),
            pl.BlockSpec((QB, N_IDX_HEADS * idx_dim), lambda s, qb, kb, kd, fs: (qb[s], ciq)),
            pl.BlockSpec((QB, LANES), lambda s, qb, kb, kd, fs: (qb[s], cikw)),
            pl.BlockSpec((nqb, idx_dim, QB), lambda s, qb, kb, kd, fs: (0, 0, 0)),
            pl.BlockSpec((d_kv, FKC), lambda s, qb, kb, kd, fs: (0, kb[s])),
            pl.BlockSpec((FKC, 2 * d_kv), lambda s, qb, kb, kd, fs: (kb[s], 0)),
            pl.BlockSpec((d_kv, QB), lambda s, qb, kb, kd, fs: (0, prev(s, qb, kb, kd, fs))),
            pl.BlockSpec((d_kv, QB), lambda s, qb, kb, kd, fs: (0, qb[s])),
            pl.BlockSpec((QB, 2 * d_kv), lambda s, qb, kb, kd, fs: (prev(s, qb, kb, kd, fs), 0)),
            pl.BlockSpec((QB, 2 * d_kv), lambda s, qb, kb, kd, fs: (qb[s], 0)),
            pl.BlockSpec((n_heads, QB, 2 * QB), lambda s, qb, kb, kd, fs: (0, 0, 0)),
        ],
        out_specs=pl.BlockSpec((QB, d_attn), lambda s, qb, kb, kd, fs: (qb[s], 0)),
        scratch_shapes=[
            pltpu.VMEM((nqb, QB, QB), I32),
            pltpu.VMEM((QB, 1), I32),
            pltpu.VMEM((QB, 1), I32),
            pltpu.VMEM((n_kv, group * QB, dh), BF16),
            pltpu.VMEM((n_kv, group * QB, LANES), F32),
            pltpu.VMEM((n_kv, group * QB, LANES), F32),
            pltpu.VMEM((n_kv, group * QB, dh), F32),
        ],
    )
    kern = functools.partial(_dsa_prompt_kernel, k_top=k_top, n_kv=n_kv, group=group, dh=dh,
                             idx_dim=idx_dim, n_pos_bits=(t - 1).bit_length())
    return pl.pallas_call(
        kern,
        out_shape=jax.ShapeDtypeStruct((t, d_attn), BF16),
        grid_spec=grid_spec,
        compiler_params=_cparams(1, 48),
        name="dsa_prompt",
    )(*tabs, proj, proj, proj, ik_t3, k_t, v_bf, k_t, k_t, v_bf, v_bf, nb)


def _dsa_sample_kernel(q_ref, iq_ref, ikw_ref, ikt_ref, kt_ref, v_ref, bias_ref, o_ref,
                       *, k_top, n_kv, group, dh, idx_dim, past, n_keys):
    s_len = q_ref.shape[0]
    lp = kt_ref.shape[1]
    scale = dh ** -0.5
    iq = iq_ref[...]
    qi = jnp.concatenate(
        [iq[:, h * idx_dim:(h + 1) * idx_dim] for h in range(N_IDX_HEADS)], axis=0).astype(BF16)
    w = ikw_ref[:, idx_dim:idx_dim + N_IDX_HEADS] * ((N_IDX_HEADS ** -0.5) * (idx_dim ** -0.5))
    sc = _index_scores(qi, w, ikt_ref[...], N_IDX_HEADS, s_len)
    row = lax.broadcasted_iota(I32, (s_len, 1), 0)
    limit = jnp.minimum(((past + row) // CHUNK + 1) * CHUNK, n_keys)
    pos = lax.broadcasted_iota(I32, (s_len, lp), 1)
    allowed = pos < limit
    key = _sort_key(jnp.where(allowed, sc, -jnp.inf))

    def count_ge(cand):
        return jnp.sum((key >= cand).astype(I32), axis=1, keepdims=True)

    thr = _kth_largest_key(count_ge, k_top, s_len)

    def count_tie_below(p):
        return jnp.sum(((key == thr) & (pos < p)).astype(I32), axis=1, keepdims=True)

    cut = _tie_cut(count_tie_below, k_top - count_ge(thr + 1), (lp - 1).bit_length(), s_len)
    sel = _select(key, pos, thr, cut) & allowed
    q = q_ref[...]
    outs = []
    for g in range(n_kv):
        qg = jnp.concatenate(
            [q[:, (g * group + r) * dh:(g * group + r + 1) * dh] for r in range(group)],
            axis=0).astype(BF16)
        lg = jnp.dot(qg, kt_ref[g * dh:(g + 1) * dh, :], preferred_element_type=F32) * scale
        lg = lg.reshape(group, s_len, lp) + bias_ref[g * group:(g + 1) * group]
        lm = jnp.where(sel[None], lg, NEG).reshape(group * s_len, lp)
        p = jnp.exp(lm - jnp.max(lm, axis=1, keepdims=True))
        den = jnp.sum(p, axis=1, keepdims=True)
        out_g = jnp.dot(p.astype(BF16), v_ref[:, g * dh:(g + 1) * dh], preferred_element_type=F32) / den
        outs += [out_g[r * s_len:(r + 1) * s_len] for r in range(group)]
    o_ref[...] = jnp.concatenate(outs, axis=1).astype(o_ref.dtype)


def _dsa_sample(proj, lay, ik_t, k_t, v_bf, rel_bias, row0, n_seq, s_len, past, n_kv, dh, d_attn,
                idx_dim):
    n_heads = d_attn // dh
    group = n_heads // n_kv
    d_kv = n_kv * dh
    lp = k_t.shape[2]
    n_keys = past + s_len
    k_top = min(TOPK_MAX, n_keys // 4)
    rel = jnp.arange(lp, dtype=I32)[None, :] - (past + jnp.arange(s_len, dtype=I32))[:, None]
    bias = jnp.moveaxis(rel_bias[_t5_bucket(rel, rel_bias.shape[0])].astype(F32), -1, 0)
    rb = row0 // s_len
    cq = lay["q"] // d_attn
    ciq = lay["iq"] // (N_IDX_HEADS * idx_dim)
    cikw = lay["ikw"] // LANES
    kern = functools.partial(_dsa_sample_kernel, k_top=k_top, n_kv=n_kv, group=group, dh=dh,
                             idx_dim=idx_dim, past=past, n_keys=n_keys)
    return pl.pallas_call(
        kern,
        out_shape=jax.ShapeDtypeStruct((n_seq * s_len, d_attn), BF16),
        grid=(n_seq,),
        in_specs=[
            pl.BlockSpec((s_len, d_attn), lambda b: (rb + b, cq)),
            pl.BlockSpec((s_len, N_IDX_HEADS * idx_dim), lambda b: (rb + b, ciq)),
            pl.BlockSpec((s_len, LANES), lambda b: (rb + b, cikw)),
            pl.BlockSpec((None, idx_dim, lp), lambda b: (b, 0, 0)),
            pl.BlockSpec((None, d_kv, lp), lambda b: (b, 0, 0)),
            pl.BlockSpec((None, lp, d_kv), lambda b: (b, 0, 0)),
            pl.BlockSpec((n_heads, s_len, lp), lambda b: (0, 0, 0)),
        ],
        out_specs=pl.BlockSpec((s_len, d_attn), lambda b: (b, 0)),
        compiler_params=_cparams(1, 32),
        name="dsa_sample",
    )(proj, proj, proj, ik_t, k_t, v_bf, bias)


def _merge_kernel(yc_ref, ya_ref, gc_ref, ga_ref, wc_ref, wa_ref, o_ref, wcb_ref, wab_ref):
    @pl.when(pl.program_id(1) == 0)
    def _():
        wcb_ref[...] = wc_ref[...].astype(BF16)
        wab_ref[...] = wa_ref[...].astype(BF16)

    bc = jnp.dot(yc_ref[...], wcb_ref[...], preferred_element_type=F32)
    ba = jnp.dot(ya_ref[...], wab_ref[...], preferred_element_type=F32)
    merged = jax.nn.sigmoid(gc_ref[...]) * bc + jax.nn.sigmoid(ga_ref[...]) * ba
    o_ref[...] = merged.astype(o_ref.dtype)


def _merge(y_conv, y_attn, proj, lay, w_bc, w_ba):
    m, dc = y_conv.shape
    da = y_attn.shape[1]
    d = w_bc.shape[1]
    tm = _pick_tile(m, 640, 16)
    tn = _pick_tile(d, 512, 128)
    cgc, cga = lay["gc"] // tn, lay["ga"] // tn
    return pl.pallas_call(
        _merge_kernel,
        out_shape=jax.ShapeDtypeStruct((m, d), BF16),
        grid=(d // tn, m // tm),
        in_specs=[pl.BlockSpec((tm, dc), lambda j, i: (i, 0)),
                  pl.BlockSpec((tm, da), lambda j, i: (i, 0)),
                  pl.BlockSpec((tm, tn), lambda j, i: (i, cgc + j)),
                  pl.BlockSpec((tm, tn), lambda j, i: (i, cga + j)),
                  pl.BlockSpec((dc, tn), lambda j, i: (0, j)),
                  pl.BlockSpec((da, tn), lambda j, i: (0, j))],
        out_specs=pl.BlockSpec((tm, tn), lambda j, i: (i, j)),
        scratch_shapes=[pltpu.VMEM((dc, tn), BF16), pltpu.VMEM((da, tn), BF16)],
        compiler_params=_cparams(2, 48),
        name="merge",
    )(y_conv, y_attn, proj, proj, w_bc, w_ba)


def _out_kernel(a_ref, x_ref, w_ref, o_ref, wb_ref):
    @pl.when(pl.program_id(1) == 0)
    def _():
        wb_ref[...] = w_ref[...].astype(BF16)

    o_ref[...] = x_ref[...] + jnp.dot(a_ref[...], wb_ref[...], preferred_element_type=F32)


def _out_proj(merged, x, w_out):
    m, k = merged.shape
    n = w_out.shape[1]
    tm = _pick_tile(m, 640, 16)
    tn = _pick_tile(n, 512, 128)
    return pl.pallas_call(
        _out_kernel,
        out_shape=jax.ShapeDtypeStruct((m, n), F32),
        grid=(n // tn, m // tm),
        in_specs=[pl.BlockSpec((tm, k), lambda j, i: (i, 0)),
                  pl.BlockSpec((tm, tn), lambda j, i: (i, j)),
                  pl.BlockSpec((k, tn), lambda j, i: (0, j))],
        out_specs=pl.BlockSpec((tm, tn), lambda j, i: (i, j)),
        scratch_shapes=[pltpu.VMEM((k, tn), BF16)],
        compiler_params=_cparams(2, 48),
        name="out_proj",
    )(merged, x, w_out)


def _split_bf16(x):
    hi = x.astype(BF16)
    lo = (x - hi.astype(F32)).astype(BF16)
    return hi, lo


def _router_kernel(x_ref, g_ref, rw_ref, rb_ref, h_ref, idx_ref, gate_ref, rank_ref, cnt_ref, carry_ref):
    @pl.when(pl.program_id(0) == 0)
    def _():
        carry_ref[...] = jnp.zeros(carry_ref.shape, F32)

    xf = x_ref[...]
    xf = xf * lax.rsqrt(jnp.mean(xf * xf, axis=-1, keepdims=True) + EPS)
    h = xf * g_ref[...]
    h_ref[...] = h
    h_hi, h_lo = _split_bf16(h)
    w_hi, w_lo = _split_bf16(rw_ref[...])
    logits = (jnp.dot(h_hi, w_hi, preferred_element_type=F32)
              + (jnp.dot(h_hi, w_lo, preferred_element_type=F32)
                 + jnp.dot(h_lo, w_hi, preferred_element_type=F32)))
    logits = logits + rb_ref[...]
    n_e = logits.shape[1]
    lane = lax.broadcasted_iota(I32, logits.shape, 1)
    vals, idxs = [], []
    cur = logits
    for _ in range(EXPERT_TOP_K):
        mx = jnp.max(cur, axis=1, keepdims=True)
        ix = jnp.min(jnp.where(cur == mx, lane, n_e), axis=1, keepdims=True)
        vals.append(mx)
        idxs.append(ix)
        cur = jnp.where(lane == ix, -jnp.inf, cur)
    top = jnp.concatenate(vals, axis=1)
    e = jnp.exp(top - vals[0])
    gate_ref[...] = e / jnp.sum(e, axis=1, keepdims=True)
    idx_ref[...] = jnp.concatenate(idxs, axis=1)

    tm = logits.shape[0]
    hot = [(lane == ix).astype(F32) for ix in idxs]
    total = hot[0]
    for oh in hot[1:]:
        total = total + oh
    ri = lax.broadcasted_iota(I32, (tm, tm), 0)
    ci = lax.broadcasted_iota(I32, (tm, tm), 1)
    tri = (ci < ri).astype(BF16)
    before = jnp.dot(tri, total.astype(BF16), preferred_element_type=F32) + carry_ref[...]
    rank_ref[...] = jnp.concatenate(
        [jnp.sum(oh * before, axis=1, keepdims=True) for oh in hot], axis=1).astype(I32)
    carry_ref[...] = carry_ref[...] + jnp.sum(total, axis=0, keepdims=True)
    cnt_ref[...] = carry_ref[...].astype(I32)


def _router(x1, g_ffn, router_w, router_b):
    m, d = x1.shape
    n_e = router_w.shape[1]
    tm = _pick_tile(m, 256, SUBLANES)
    row_spec = pl.BlockSpec((tm, EXPERT_TOP_K), lambda i: (i, 0))
    return pl.pallas_call(
        _router_kernel,
        out_shape=(jax.ShapeDtypeStruct((m, d), F32),
                   jax.ShapeDtypeStruct((m, EXPERT_TOP_K), I32),
                   jax.ShapeDtypeStruct((m, EXPERT_TOP_K), F32),
                   jax.ShapeDtypeStruct((m, EXPERT_TOP_K), I32),
                   jax.ShapeDtypeStruct((1, n_e), I32)),
        grid=(m // tm,),
        in_specs=[pl.BlockSpec((tm, d), lambda i: (i, 0)),
                  pl.BlockSpec((1, d), lambda i: (0, 0)),
                  pl.BlockSpec((d, n_e), lambda i: (0, 0)),
                  pl.BlockSpec((1, n_e), lambda i: (0, 0))],
        out_specs=(pl.BlockSpec((tm, d), lambda i: (i, 0)), row_spec, row_spec, row_spec,
                   pl.BlockSpec((1, n_e), lambda i: (0, 0))),
        scratch_shapes=[pltpu.VMEM((1, n_e), F32)],
        compiler_params=_cparams(1, 48),
        name="router",
    )(x1, g_ffn.reshape(1, d), router_w, router_b.reshape(1, n_e))


def _row_copy(src_hbm, buf, sem, src_row, dst_row):
    return pltpu.make_async_copy(src_hbm.at[src_row], buf.at[dst_row], sem)


RUN_BLOCKS = 5
GATHER_ROWS = 128


def _up_kernel(re_ref, rc_ref,
               tok_hbm, h_hbm, wg_ref, wl_ref, bg_ref, bl_ref, o_ref,
               xs_ref, stage_ref, tok_smem, wgb_ref, wlb_ref, sem_idx, sem):
    j = pl.program_id(0)
    cnt = rc_ref[j]
    tm = xs_ref.shape[1]
    n_burst = tm // GATHER_ROWS

    def gather_block(r, carry):
        cp = pltpu.make_async_copy(tok_hbm.at[j * RUN_BLOCKS + r], tok_smem, sem_idx)
        cp.start()
        cp.wait()

        def issue(c):
            def body(i, c2):
                _row_copy(h_hbm, stage_ref.at[c % 2], sem.at[c % 2],
                          tok_smem[c * GATHER_ROWS + i], i).start()
                return c2
            lax.fori_loop(0, GATHER_ROWS, body, 0, unroll=8)

        def drain(c):
            def body(i, c2):
                _row_copy(h_hbm, stage_ref.at[c % 2], sem.at[c % 2], 0, i).wait()
                return c2
            lax.fori_loop(0, GATHER_ROWS, body, 0, unroll=8)

        issue(0)
        for c in range(n_burst):
            if c + 1 < n_burst:
                issue(c + 1)
            drain(c)
            xs_ref[r, c * GATHER_ROWS:(c + 1) * GATHER_ROWS, :] = stage_ref[c % 2].astype(BF16)
        return carry

    @pl.when(pl.program_id(1) == 0)
    def _gather():
        lax.fori_loop(0, cnt, gather_block, 0)

    @pl.when(cnt > 0)
    def _cast():
        wgb_ref[...] = wg_ref[...].astype(BF16)
        wlb_ref[...] = wl_ref[...].astype(BF16)

    def compute_block(r, carry):
        x = xs_ref[r]
        hg = jnp.dot(x, wgb_ref[...], preferred_element_type=F32) + bg_ref[...]
        hl = jnp.dot(x, wlb_ref[...], preferred_element_type=F32) + bl_ref[...]
        x_glu = jnp.minimum(hg, SWIGLU_LIMIT)
        x_lin = jnp.clip(hl, -SWIGLU_LIMIT, SWIGLU_LIMIT)
        act = x_glu * jax.nn.sigmoid(SWIGLU_ALPHA * x_glu) * (x_lin + 1.0)
        o_ref[pl.ds(pl.multiple_of(r * tm, tm), tm), :] = act.astype(o_ref.dtype)
        return carry

    def zero_block(r, carry):
        o_ref[pl.ds(pl.multiple_of(r * tm, tm), tm), :] = jnp.zeros((tm, o_ref.shape[1]), o_ref.dtype)
        return carry

    lax.fori_loop(0, cnt, compute_block, 0)
    lax.fori_loop(cnt, RUN_BLOCKS, zero_block, 0)


def _expert_up(h2, row_tok, run_e, run_cnt, w_up, b_up, tm, tf):
    d = h2.shape[1]
    n_e, _, f2 = w_up.shape
    f = f2 // 2
    nf = f // tf
    n_rows = row_tok.shape[0]
    n_runs = run_e.shape[0]
    wtile = lambda half: (lambda j, t, re, rc: (re[j], 0, half * nf + jnp.where(rc[j] > 0, t, nf - 1)))
    grid_spec = pltpu.PrefetchScalarGridSpec(
        num_scalar_prefetch=2,
        grid=(n_runs, nf),
        in_specs=[
            pl.BlockSpec(memory_space=pl.ANY),
            pl.BlockSpec(memory_space=pl.ANY),
            pl.BlockSpec((None, d, tf), wtile(0)),
            pl.BlockSpec((None, d, tf), wtile(1)),
            pl.BlockSpec((None, 1, tf), wtile(0)),
            pl.BlockSpec((None, 1, tf), wtile(1)),
        ],
        out_specs=pl.BlockSpec((RUN_BLOCKS * tm, tf), lambda j, t, re, rc: (j, t)),
        scratch_shapes=[pltpu.VMEM((RUN_BLOCKS, tm, d), BF16),
                        pltpu.VMEM((2, GATHER_ROWS, d), h2.dtype),
                        pltpu.SMEM((tm,), I32),
                        pltpu.VMEM((d, tf), BF16), pltpu.VMEM((d, tf), BF16),
                        pltpu.SemaphoreType.DMA(()),
                        pltpu.SemaphoreType.DMA((2,))],
    )
    return pl.pallas_call(
        _up_kernel,
        out_shape=jax.ShapeDtypeStruct((n_rows, f), BF16),
        grid_spec=grid_spec,
        compiler_params=_cparams(2, 56),
        name="moe_up",
    )(run_e, run_cnt, row_tok.reshape(n_rows // tm, tm), h2, w_up, w_up,
      b_up.reshape(n_e, 1, f2), b_up.reshape(n_e, 1, f2))


def _down_kernel(re_ref, rc_ref, a_hbm, w_ref, b_ref, o_ref, as_ref, wb_ref, sem):
    j = pl.program_id(0)
    cnt = rc_ref[j]
    tm = as_ref.shape[1]

    def block_copy(r):
        row0 = pl.multiple_of((j * RUN_BLOCKS + r) * tm, tm)
        return pltpu.make_async_copy(a_hbm.at[pl.ds(row0, tm)], as_ref.at[r], sem)

    @pl.when(pl.program_id(1) == 0)
    def _load():
        def start(r, c):
            block_copy(r).start()
            return c

        def wait(r, c):
            block_copy(r).wait()
            return c

        lax.fori_loop(0, cnt, start, 0)
        lax.fori_loop(0, cnt, wait, 0)

    @pl.when(cnt > 0)
    def _cast():
        wb_ref[...] = w_ref[...].astype(BF16)

    def compute_block(r, carry):
        y = jnp.dot(as_ref[r], wb_ref[...], preferred_element_type=F32) + b_ref[...]
        o_ref[pl.ds(pl.multiple_of(r * tm, tm), tm), :] = y
        return carry

    def zero_block(r, carry):
        o_ref[pl.ds(pl.multiple_of(r * tm, tm), tm), :] = jnp.zeros((tm, o_ref.shape[1]), o_ref.dtype)
        return carry

    lax.fori_loop(0, cnt, compute_block, 0)
    lax.fori_loop(cnt, RUN_BLOCKS, zero_block, 0)


def _expert_down(act, run_e, run_cnt, w_down, b_down, tm, tn):
    n_rows, f = act.shape
    n_e, _, d = w_down.shape
    nt = d // tn
    n_runs = run_e.shape[0]
    wtile = lambda j, t, re, rc: (re[j], 0, jnp.where(rc[j] > 0, t, nt - 1))
    grid_spec = pltpu.PrefetchScalarGridSpec(
        num_scalar_prefetch=2,
        grid=(n_runs, nt),
        in_specs=[
            pl.BlockSpec(memory_space=pl.ANY),
            pl.BlockSpec((None, f, tn), wtile),
            pl.BlockSpec((None, 1, tn), wtile),
        ],
        out_specs=pl.BlockSpec((RUN_BLOCKS * tm, tn), lambda j, t, re, rc: (j, t)),
        scratch_shapes=[pltpu.VMEM((RUN_BLOCKS, tm, f), BF16),
                        pltpu.VMEM((f, tn), BF16),
                        pltpu.SemaphoreType.DMA(())],
    )
    return pl.pallas_call(
        _down_kernel,
        out_shape=jax.ShapeDtypeStruct((n_rows, d), F32),
        grid_spec=grid_spec,
        compiler_params=_cparams(2, 56),
        name="moe_down",
    )(run_e, run_cnt, act, w_down, b_down.reshape(n_e, 1, d))


def _combine_kernel(pos_hbm, y_hbm, x_ref, gt_ref, g_ref, o_ref, pos_smem, buf, sem_idx, sem):
    i = pl.program_id(0)
    n_steps = pl.num_programs(0)
    tb = x_ref.shape[0]
    n = pos_smem.shape[1]
    cur = i % 2
    nxt = 1 - cur

    def idx_copy(step, sl):
        return pltpu.make_async_copy(pos_hbm.at[step], pos_smem.at[sl], sem_idx.at[sl])

    def issue_rows(sl):
        def body(r, c):
            _row_copy(y_hbm, buf.at[sl], sem.at[sl], pos_smem[sl, r], r).start()
            return c
        lax.fori_loop(0, n, body, 0, unroll=8)

    @pl.when(i == 0)
    def _():
        idx_copy(0, 0).start()
        idx_copy(0, 0).wait()
        issue_rows(0)

        @pl.when(n_steps > 1)
        def _():
            idx_copy(1, 1).start()

    @pl.when(i + 1 < n_steps)
    def _():
        idx_copy(i + 1, nxt).wait()
        issue_rows(nxt)

    @pl.when(i + 2 < n_steps)
    def _():
        idx_copy(i + 2, cur).start()

    def drain(r, c):
        _row_copy(y_hbm, buf.at[cur], sem.at[cur], 0, r).wait()
        return c

    lax.fori_loop(0, n, drain, 0, unroll=8)
    gt = gt_ref[...]
    moe = None
    for j in range(EXPERT_TOP_K):
        term = buf[cur, j * tb:(j + 1) * tb, :] * gt[:, j:j + 1]
        moe = term if moe is None else moe + term
    xf = x_ref[...] + moe
    xf = xf * lax.rsqrt(jnp.mean(xf * xf, axis=-1, keepdims=True) + EPS)
    o_ref[...] = xf * g_ref[...]


def _combine(y_rows, pos, gates, x1, g_final, tb):
    m, d = x1.shape
    nb = m // tb
    n = EXPERT_TOP_K * tb
    pos_blocks = pos.reshape(nb, tb, EXPERT_TOP_K).transpose(0, 2, 1).reshape(nb, n)
    return pl.pallas_call(
        _combine_kernel,
        out_shape=jax.ShapeDtypeStruct((m, d), F32),
        grid=(nb,),
        in_specs=[pl.BlockSpec(memory_space=pl.ANY), pl.BlockSpec(memory_space=pl.ANY),
                  pl.BlockSpec((tb, d), lambda i: (i, 0)),
                  pl.BlockSpec((tb, EXPERT_TOP_K), lambda i: (i, 0)),
                  pl.BlockSpec((1, d), lambda i: (0, 0))],
        out_specs=pl.BlockSpec((tb, d), lambda i: (i, 0)),
        scratch_shapes=[pltpu.SMEM((2, n), I32),
                        pltpu.VMEM((2, n, d), F32),
                        pltpu.SemaphoreType.DMA((2,)),
                        pltpu.SemaphoreType.DMA((2,))],
        compiler_params=_cparams(1, 32),
        name="moe_combine",
    )(pos_blocks, y_rows, x1, gates, g_final.reshape(1, d))


def _routing_tables(top_idx, rank, counts, tm):
    t = top_idx.shape[0]
    n = t * EXPERT_TOP_K
    n_experts = counts.shape[0]
    n_runs = (-(-n // tm) + n_experts) // RUN_BLOCKS + n_experts
    nblk_e = (counts + tm - 1) // tm
    nrun_e = (nblk_e + RUN_BLOCKS - 1) // RUN_BLOCKS
    run_end = jnp.cumsum(nrun_e)
    run_start = run_end - nrun_e
    experts = jnp.arange(n_experts, dtype=I32)
    row0 = run_start * (RUN_BLOCKS * tm)
    hot = top_idx[:, :, None] == experts[None, None, :]
    pos = jnp.sum(jnp.where(hot, row0[None, None, :], 0), axis=-1) + rank
    tok = jnp.broadcast_to(jnp.arange(t, dtype=I32)[:, None], pos.shape)
    row_tok = jnp.zeros((n_runs * RUN_BLOCKS * tm,), I32).at[pos.reshape(n)].set(tok.reshape(n))
    j = jnp.arange(n_runs, dtype=I32)
    e_j = jnp.minimum(jnp.sum((run_end[None, :] <= j[:, None]).astype(I32), axis=1), n_experts - 1)
    sel = e_j[:, None] == experts[None, :]
    pick = lambda a: jnp.sum(jnp.where(sel, a[None, :], 0), axis=1)
    in_e = j - pick(run_start)
    run_cnt = jnp.clip(pick(nblk_e) - in_e * RUN_BLOCKS, 0, RUN_BLOCKS)
    run_cnt = jnp.where(j < run_end[-1], run_cnt, 0).astype(I32)
    last_e = jnp.sum(jnp.where(j == run_end[-1] - 1, e_j, 0))
    run_e = jnp.where(run_cnt > 0, e_j, last_e).astype(I32)
    return pos, row_tok, run_e, run_cnt


def _proj_layout(d, dc, d_attn, d_kv, idx_dim):
    iq_w = N_IDX_HEADS * idx_dim
    assert idx_dim + N_IDX_HEADS <= LANES
    segs = [("gc", d), ("ga", d), ("u", dc), ("c", dc), ("b", dc), ("q", d_attn),
            ("iq", iq_w), ("k", d_kv), ("v", d_kv), ("ikw", LANES)]
    segs.sort(key=lambda s: -s[1])
    lay, off = {}, 0
    for name, width in segs:
        assert off % width == 0, (name, off, width)
        lay[name] = off
        off += width
    lay["total"] = -(-off // 512) * 512
    return lay


def _relayout_w_in(w, lay, dc, d_attn, d_kv, idx_dim, d):
    widths = (dc, dc, dc, d_attn, d_kv, d_kv, N_IDX_HEADS * idx_dim, idx_dim, N_IDX_HEADS, d, d)
    names = ("u", "c", "b", "q", "k", "v", "iq", "ik", "iw", "gc", "ga")
    src, o = {}, 0
    for nme, wd in zip(names, widths):
        src[nme] = (o, wd)
        o += wd
    assert o == w.shape[1]
    out = jnp.zeros((w.shape[0], lay["total"]), w.dtype)
    for nme in ("gc", "ga", "u", "c", "b", "q", "iq", "k", "v"):
        s0, wd = src[nme]
        out = lax.dynamic_update_slice(out, w[:, s0:s0 + wd], (0, lay[nme]))
    s0 = src["ik"][0]
    out = lax.dynamic_update_slice(out, w[:, s0:s0 + idx_dim + N_IDX_HEADS], (0, lay["ikw"]))
    return out


def kernel(x_prompt, x_sample, cache_k, cache_v, cache_idx_k, state_conv, g_mix, w_in, conv_w,
           w_branch_conv, w_branch_attn, w_out, rel_bias, g_ffn, router_w, router_b, w_up, b_up,
           w_down, b_down, g_final):
    depth = w_in.shape[0]
    assert depth == 1, "one trunk layer"
    bp, t, d = x_prompt.shape
    assert bp == 1
    n_seq, s_len, _ = x_sample.shape
    past, n_kv, dh = cache_k.shape[2:5]
    idx_dim = cache_idx_k.shape[-1]
    dc = conv_w.shape[-1]
    d_attn = w_branch_attn.shape[1]
    d_kv = n_kv * dh
    n_experts = router_w.shape[-1]
    assert t % FKC == 0 and s_len % 16 == 0

    lay = _proj_layout(d, dc, d_attn, d_kv, idx_dim)
    w_in_r = _relayout_w_in(w_in[0], lay, dc, d_attn, d_kv, idx_dim, d)

    x_all = jnp.concatenate([x_prompt.reshape(t, d), x_sample.reshape(n_seq * s_len, d)], axis=0)
    h = _rmsnorm(x_all, g_mix[0], BF16)
    proj = _matmul(h, w_in_r)

    k_new = proj[:, lay["k"]:lay["k"] + d_kv]
    v_new = proj[:, lay["v"]:lay["v"] + d_kv]
    ik_new = proj[:, lay["ikw"]:lay["ikw"] + idx_dim]

    y_conv, st_p, st_s = _short_conv(proj, lay, conv_w[0], state_conv[0], t, n_seq, s_len)

    k_t = k_new[:t].T.astype(BF16)
    v_bf = v_new[:t].astype(BF16)
    ik_t3 = ik_new[:t].reshape(t // QB, QB, idx_dim).transpose(0, 2, 1).astype(BF16)
    ya_p = _dsa_prompt(proj, lay, ik_t3, k_t, v_bf, rel_bias, t, n_kv, dh, d_attn, idx_dim)

    n_keys = past + s_len
    lp = -(-n_keys // LANES) * LANES

    def with_cache(cache, new, width):
        full = jnp.concatenate([cache.reshape(n_seq, past, width),
                                new[t:].reshape(n_seq, s_len, width)], axis=1)
        return jnp.pad(full, ((0, 0), (0, lp - n_keys), (0, 0))).astype(BF16)

    k_s = with_cache(cache_k[0], k_new, d_kv)
    v_s = with_cache(cache_v[0], v_new, d_kv)
    ik_s = with_cache(cache_idx_k[0], ik_new, idx_dim)
    ya_s = _dsa_sample(proj, lay, ik_s.transpose(0, 2, 1), k_s.transpose(0, 2, 1), v_s, rel_bias,
                       t, n_seq, s_len, past, n_kv, dh, d_attn, idx_dim)
    y_attn = jnp.concatenate([ya_p, ya_s], axis=0)

    merged = _merge(y_conv, y_attn, proj, lay, w_branch_conv[0], w_branch_attn[0])
    x1 = _out_proj(merged, x_all, w_out[0])

    h2, top_idx, gates, rank, counts = _router(x1, g_ffn[0], router_w[0], router_b[0])
    tm_e = 512 if (x_all.shape[0] * EXPERT_TOP_K) >= 16 * 512 else 128
    pos, row_tok, run_e, run_cnt = _routing_tables(top_idx, rank, counts.reshape(n_experts), tm_e)
    tf = _pick_tile(w_down.shape[2], 256, LANES)
    tn = _pick_tile(d, 512, LANES)
    act = _expert_up(h2, row_tok, run_e, run_cnt, w_up[0], b_up[0], tm_e, tf)
    y_rows = _expert_down(act, run_e, run_cnt, w_down[0], b_down[0], tm_e, tn)
    tb = _pick_tile(x_all.shape[0], 64, SUBLANES)
    y_all = _combine(y_rows, pos, gates, x1, g_final, tb)

    y_prompt = y_all[:t].reshape(1, t, d)
    y_sample = y_all[t:].reshape(n_seq, s_len, d)
    return (y_prompt, y_sample,
            k_new[:t].reshape(1, 1, t, n_kv, dh), v_new[:t].reshape(1, 1, t, n_kv, dh),
            ik_new[:t].reshape(1, 1, t, idx_dim), st_p.reshape(1, 1, 2, dc),
            k_new[t:].reshape(1, n_seq, s_len, n_kv, dh), v_new[t:].reshape(1, n_seq, s_len, n_kv, dh),
            ik_new[t:].reshape(1, n_seq, s_len, idx_dim), st_s.reshape(1, n_seq, 2, dc))
```

```python
import functools
import math

import numpy as np
import jax
import jax.numpy as jnp
from jax import lax
from jax.experimental import pallas as pl
from jax.experimental.pallas import tpu as pltpu

F32 = jnp.float32
BF16 = jnp.bfloat16
I32 = jnp.int32

CHUNK = 64
EPS = 1e-6
TOPK_MAX = 256
N_IDX_HEADS = 16
MAX_DISTANCE = 128
EXPERT_TOP_K = 4
SWIGLU_LIMIT = 7.0
SWIGLU_ALPHA = 1.702

LANES = 128
SUBLANES = 8
VMEM_BYTES_V7X = 64 * 1024 * 1024

QB = 128
KC = 512
SUB = KC // QB
FKC = 1024
FSUB = FKC // QB
SLAB = 32
LOG2E = 1.4426950408889634
NEG = -1e30
INT_MIN = -(2 ** 31)
INT_MAX = 2 ** 31 - 1


def _cparams(n_axes, vmem_mb):
    return pltpu.CompilerParams(
        dimension_semantics=("arbitrary",) * n_axes,
        vmem_limit_bytes=vmem_mb * 1024 * 1024,
    )


def _pick_tile(n, max_tile, quantum):
    best = None
    t = quantum
    while t <= min(n, max_tile):
        if n % t == 0:
            best = t
        t += quantum
    assert best is not None, (n, max_tile, quantum)
    return best


def _rmsnorm_kernel(x_ref, g_ref, o_ref):
    xf = x_ref[...]
    xf = xf * lax.rsqrt(jnp.mean(xf * xf, axis=-1, keepdims=True) + EPS)
    o_ref[...] = (xf * g_ref[...]).astype(o_ref.dtype)


def _rmsnorm(x, g, out_dtype):
    m, d = x.shape
    tm = _pick_tile(m, 256, SUBLANES * 2)
    return pl.pallas_call(
        _rmsnorm_kernel,
        out_shape=jax.ShapeDtypeStruct((m, d), out_dtype),
        grid=(m // tm,),
        in_specs=[pl.BlockSpec((tm, d), lambda i: (i, 0)),
                  pl.BlockSpec((1, d), lambda i: (0, 0))],
        out_specs=pl.BlockSpec((tm, d), lambda i: (i, 0)),
        compiler_params=_cparams(1, 32),
        name="rmsnorm",
    )(x, g.reshape(1, d))


def _mm_kernel(a_ref, w_ref, o_ref, wb_ref):
    @pl.when(pl.program_id(1) == 0)
    def _():
        wb_ref[...] = w_ref[...].astype(BF16)

    o_ref[...] = jnp.dot(a_ref[...], wb_ref[...], preferred_element_type=F32)


def _matmul(a, w):
    m, k = a.shape
    n = w.shape[1]
    tm = _pick_tile(m, 1280, 16)
    tn = _pick_tile(n, 512, 128)
    return pl.pallas_call(
        _mm_kernel,
        out_shape=jax.ShapeDtypeStruct((m, n), F32),
        grid=(n // tn, m // tm),
        in_specs=[pl.BlockSpec((tm, k), lambda j, i: (i, 0)),
                  pl.BlockSpec((k, tn), lambda j, i: (0, j))],
        out_specs=pl.BlockSpec((tm, tn), lambda j, i: (i, j)),
        scratch_shapes=[pltpu.VMEM((k, tn), BF16)],
        compiler_params=_cparams(2, 56),
        name="in_proj",
    )(a, w)


def _conv_prompt_kernel(u_ref, c_ref, b_ref, up_ref, cp_ref, w_ref, y_ref, st_ref):
    i = pl.program_id(0)
    tm = u_ref.shape[0]
    v = c_ref[...] * u_ref[...]
    hv = cp_ref[...] * up_ref[...]
    hv = jnp.where(i > 0, hv, 0.0)
    vp = jnp.concatenate([hv, v], axis=0)
    w = w_ref[...]
    h = SUBLANES
    y = w[0:1] * vp[h - 2:h - 2 + tm] + w[1:2] * vp[h - 1:h - 1 + tm]
    y = y + w[2:3] * v
    y_ref[...] = (b_ref[...] * y).astype(y_ref.dtype)

    @pl.when(i == pl.num_programs(0) - 1)
    def _():
        st_ref[...] = v[tm - 2:tm]


def _conv_sample_kernel(u_ref, c_ref, b_ref, s_ref, w_ref, y_ref, st_ref):
    t = u_ref.shape[0]
    v = c_ref[...] * u_ref[...]
    vp = jnp.concatenate([s_ref[...], v], axis=0)
    w = w_ref[...]
    y = w[0:1] * vp[0:t] + w[1:2] * vp[1:t + 1]
    y = y + w[2:3] * vp[2:t + 2]
    y_ref[...] = (b_ref[...] * y).astype(y_ref.dtype)
    st_ref[...] = vp[t:t + 2]


def _short_conv(proj, lay, conv_w, state, t_prompt, n_seq, s_len):
    dc = conv_w.shape[1]
    cu, cc, cb = lay["u"] // dc, lay["c"] // dc, lay["b"] // dc
    tm = _pick_tile(t_prompt, 256, 16)
    hb = tm // SUBLANES
    y_p, st_p = pl.pallas_call(
        _conv_prompt_kernel,
        out_shape=(jax.ShapeDtypeStruct((t_prompt, dc), BF16),
                   jax.ShapeDtypeStruct((2, dc), F32)),
        grid=(t_prompt // tm,),
        in_specs=[pl.BlockSpec((tm, dc), lambda i: (i, cu)),
                  pl.BlockSpec((tm, dc), lambda i: (i, cc)),
                  pl.BlockSpec((tm, dc), lambda i: (i, cb)),
                  pl.BlockSpec((SUBLANES, dc), lambda i: (jnp.maximum(i * hb - 1, 0), cu)),
                  pl.BlockSpec((SUBLANES, dc), lambda i: (jnp.maximum(i * hb - 1, 0), cc)),
                  pl.BlockSpec((3, dc), lambda i: (0, 0))],
        out_specs=(pl.BlockSpec((tm, dc), lambda i: (i, 0)),
                   pl.BlockSpec((2, dc), lambda i: (0, 0))),
        compiler_params=_cparams(1, 32),
        name="conv_prompt",
    )(proj, proj, proj, proj, proj, conv_w)

    rb = t_prompt // s_len
    y_s, st_s = pl.pallas_call(
        _conv_sample_kernel,
        out_shape=(jax.ShapeDtypeStruct((n_seq * s_len, dc), BF16),
                   jax.ShapeDtypeStruct((n_seq, 2, dc), F32)),
        grid=(n_seq,),
        in_specs=[pl.BlockSpec((s_len, dc), lambda b: (rb + b, cu)),
                  pl.BlockSpec((s_len, dc), lambda b: (rb + b, cc)),
                  pl.BlockSpec((s_len, dc), lambda b: (rb + b, cb)),
                  pl.BlockSpec((None, 2, dc), lambda b: (b, 0, 0)),
                  pl.BlockSpec((3, dc), lambda b: (0, 0))],
        out_specs=(pl.BlockSpec((s_len, dc), lambda b: (b, 0)),
                   pl.BlockSpec((None, 2, dc), lambda b: (b, 0, 0))),
        compiler_params=_cparams(1, 32),
        name="conv_sample",
    )(proj, proj, proj, state, conv_w)
    return jnp.concatenate([y_p, y_s], axis=0), st_p, st_s


def _sort_key(score):
    score = jnp.where(score == 0.0, 0.0, score)
    bits = lax.bitcast_convert_type(score, I32)
    return jnp.where(bits < 0, bits ^ 0x7FFFFFFF, bits)


def _kth_largest_key(count_ge, k_top, rows, n_keys):
    zero = jnp.zeros((rows, 1), I32)
    c0 = count_ge(zero)
    pos_side = c0 >= k_top
    base = jnp.where(pos_side, zero, jnp.full((rows, 1), INT_MIN, I32))
    cb = jnp.where(pos_side, c0, n_keys)

    def unsettled(cb):
        return jnp.max((cb != k_top).astype(I32))

    def cond(c):
        it, _, _, todo = c
        return (it < 31) & (todo > 0)

    def body(c):
        it, base, cb, _ = c
        cand = base | (jnp.int32(1) << (30 - it))
        cnt = count_ge(cand)
        take = cnt >= k_top
        cb = jnp.where(take, cnt, cb)
        return it + 1, jnp.where(take, cand, base), cb, unsettled(cb)

    _, base, cb, _ = lax.while_loop(cond, body, (jnp.int32(0), base, cb, unsettled(cb)))
    return base, cb


def _tie_cut(count_tie_below, need, n_bits, rows):
    def body(it, p):
        cand = p | (jnp.int32(1) << (n_bits - 1 - it))
        return jnp.where(count_tie_below(cand) < need, cand, p)

    return lax.fori_loop(0, n_bits, body, jnp.zeros((rows, 1), I32))


def _select(key, pos, thr, cut):
    return (key > thr) | ((key == thr) & (pos <= cut))


_NT = (((1,), (1,)), ((), ()))


def _index_scores(qi, w, ikc, n_heads, rows, keys_major=False):
    if keys_major:
        s_all = lax.dot_general(qi, ikc, _NT, preferred_element_type=F32)
    else:
        s_all = jnp.dot(qi, ikc, preferred_element_type=F32)
    acc = None
    for h in range(n_heads):
        term = w[:, h:h + 1] * jnp.maximum(s_all[h * rows:(h + 1) * rows], 0.0)
        acc = term if acc is None else acc + term
    return acc


def _lane_fold(x, op):
    out = x[:, 0:LANES]
    for j in range(1, x.shape[1] // LANES):
        out = op(out, x[:, j * LANES:(j + 1) * LANES])
    return out


def _softmax_step(lg, madd, bias_slab, v_g, m_ref, l_ref, acc_ref, g):
    rows_g = lg.shape[0]
    dh = acc_ref.shape[-1]

    def logits(r0):
        q_off = r0 % QB
        x = lg[r0:r0 + SLAB] + madd[q_off:q_off + SLAB]
        b = bias_slab(r0 // QB, q_off)
        return x if b is None else x + b

    part = jnp.concatenate([_lane_fold(logits(r0), jnp.maximum) for r0 in range(0, rows_g, SLAB)], axis=0)
    m_old = m_ref[g]
    m_new = jnp.maximum(m_old, jnp.broadcast_to(jnp.max(part, axis=1, keepdims=True), m_old.shape))
    alpha = jnp.exp2(m_old - m_new)
    ps = []
    for r0 in range(0, rows_g, SLAB):
        x = logits(r0)
        m_s = m_new[r0:r0 + SLAB]
        ps.append(jnp.concatenate(
            [jnp.exp2(x[:, j * LANES:(j + 1) * LANES] - m_s) for j in range(x.shape[1] // LANES)],
            axis=1).astype(BF16))
    pv = jnp.dot(jnp.concatenate(ps, axis=0), v_g, preferred_element_type=F32)
    m_ref[g] = m_new
    l_ref[g] = alpha * l_ref[g] + pv[:, dh:]
    acc_ref[g] = alpha * acc_ref[g] + pv[:, :dh]


def _dsa_prompt_kernel(qb_ref, kb_ref, kind_ref, first_ref,
                       q_ref, iq_ref, ikw_ref, ikt_ref, ktf_ref, vf_ref,
                       kta_ref, ktb_ref, va_ref, vb_ref, nb_ref,
                       o_ref,
                       sc_ref, thr_ref, cut_ref, qs_ref, m_ref, l_ref, acc_ref,
                       *, k_top, n_kv, group, dh, idx_dim, n_pos_bits):
    s = pl.program_id(0)
    qb = qb_ref[s]
    kb = kb_ref[s]
    q0 = qb * QB
    coef = (dh ** -0.5) * LOG2E
    row = lax.broadcasted_iota(I32, (QB, 1), 0)
    limit = q0 + (row // CHUNK + 1) * CHUNK

    @pl.when(first_ref[s] == 1)
    def _setup():
        iq = iq_ref[...]
        qi = jnp.concatenate(
            [iq[:, h * idx_dim:(h + 1) * idx_dim] for h in range(N_IDX_HEADS)], axis=0).astype(BF16)
        w = ikw_ref[:, idx_dim:idx_dim + N_IDX_HEADS] * ((N_IDX_HEADS ** -0.5) * (idx_dim ** -0.5))
        n_sw = qb // SUB + 1

        def score_body(c, carry):
            ikc = jnp.concatenate([ikt_ref[c * SUB + j] for j in range(SUB)], axis=1)
            sc = _index_scores(qi, w, ikc, N_IDX_HEADS, QB)
            pos = c * KC + lax.broadcasted_iota(I32, (QB, KC), 1)
            key = _sort_key(jnp.where(pos < limit, sc, -jnp.inf))
            for j in range(SUB):
                sc_ref[c * SUB + j] = key[:, j * QB:(j + 1) * QB]
            return carry

        lax.fori_loop(0, n_sw, score_body, 0)

        def count_ge(cand):
            def body(c, acc):
                for j in range(SUB):
                    acc = acc + (sc_ref[c * SUB + j] >= cand).astype(I32)
                return acc
            acc = lax.fori_loop(0, n_sw, body, jnp.zeros((QB, QB), I32))
            return jnp.sum(acc, axis=1, keepdims=True)

        thr_k, cnt_k = _kth_largest_key(count_ge, k_top, QB, n_sw * KC)
        thr_ref[...] = thr_k
        cut_ref[...] = jnp.full((QB, 1), INT_MAX, I32)
        excess = cnt_k > k_top

        @pl.when(jnp.max(excess.astype(I32)) > 0)
        def _ties():
            def count_tie_below(p):
                def body(c, acc):
                    for j in range(SUB):
                        pos = (c * SUB + j) * QB + lax.broadcasted_iota(I32, (QB, QB), 1)
                        acc = acc + ((sc_ref[c * SUB + j] == thr_k) & (pos < p)).astype(I32)
                    return acc
                acc = lax.fori_loop(0, n_sw, body, jnp.zeros((QB, QB), I32))
                return jnp.sum(acc, axis=1, keepdims=True)

            need = k_top - count_ge(thr_k + 1)
            cut = _tie_cut(count_tie_below, need, n_pos_bits, QB)
            cut_ref[...] = jnp.where(excess, cut, INT_MAX)

        q = q_ref[...] * coef
        for g in range(n_kv):
            qs_ref[g] = jnp.concatenate(
                [q[:, (g * group + r) * dh:(g * group + r + 1) * dh] for r in range(group)],
                axis=0).astype(BF16)
        m_ref[...] = jnp.full(m_ref.shape, NEG, F32)
        l_ref[...] = jnp.zeros(l_ref.shape, F32)
        acc_ref[...] = jnp.zeros(acc_ref.shape, F32)

    thr = thr_ref[...]
    cut = cut_ref[...]

    @pl.when(kind_ref[s] == 0)
    def _far():
        pos = kb * FKC + lax.broadcasted_iota(I32, (QB, FKC), 1)
        key = jnp.concatenate([sc_ref[jnp.minimum(kb * FSUB + j, qb)] for j in range(FSUB)], axis=1)
        sel = _select(key, pos, thr, cut)
        sel = sel & (pos < q0 - QB)
        madd = jnp.where(sel, 0.0, NEG)
        lgs = [jnp.dot(qs_ref[g], ktf_ref[g * dh:(g + 1) * dh, :], preferred_element_type=F32)
               for g in range(n_kv)]
        for g in range(n_kv):
            _softmax_step(lgs[g], madd, lambda hh, q_off: None,
                          vf_ref[:, g * 2 * dh:(g + 1) * 2 * dh], m_ref, l_ref, acc_ref, g)

    @pl.when(kind_ref[s] == 1)
    def _window():
        w_keys = 2 * QB
        pos = q0 - QB + lax.broadcasted_iota(I32, (QB, w_keys), 1)
        key = jnp.concatenate([sc_ref[jnp.maximum(qb - 1, 0)], sc_ref[qb]], axis=1)
        sel = _select(key, pos, thr, cut) & (pos >= 0) & (pos < limit)
        kt = jnp.concatenate([kta_ref[...], ktb_ref[...]], axis=1)
        v = jnp.concatenate([va_ref[...], vb_ref[...]], axis=0)
        madd = jnp.where(sel, 0.0, NEG)
        outs = []
        for g in range(n_kv):
            lg = jnp.dot(qs_ref[g], kt[g * dh:(g + 1) * dh, :], preferred_element_type=F32)
            _softmax_step(lg, madd,
                          lambda hh, q_off, g=g: nb_ref[g * group + hh, q_off:q_off + SLAB, :],
                          v[:, g * 2 * dh:(g + 1) * 2 * dh], m_ref, l_ref, acc_ref, g)
            out_g = acc_ref[g] / l_ref[g]
            outs += [out_g[r * QB:(r + 1) * QB] for r in range(group)]
        o_ref[...] = jnp.concatenate(outs, axis=1).astype(o_ref.dtype)


def _t5_bucket(rel, n_buckets):
    nb = n_buckets // 2
    max_exact = nb // 2
    n = jnp.abs(rel)
    nf = jnp.maximum(n, 1).astype(F32)
    large = max_exact + (jnp.log(nf / max_exact) / math.log(MAX_DISTANCE / max_exact)
                         * (nb - max_exact)).astype(I32)
    large = jnp.minimum(large, nb - 1)
    return jnp.where(rel > 0, nb, 0) + jnp.where(n < max_exact, n, large)


def _bias_lookup(rel_bias, bucket):
    out = jnp.zeros(bucket.shape + (rel_bias.shape[1],), F32)
    for b in range(rel_bias.shape[0]):
        out = jnp.where((bucket == b)[..., None], rel_bias[b].astype(F32), out)
    return out


def _dsa_prompt(proj, lay, ik_t3, k_t, v_bf, rel_bias, t, n_kv, dh, d_attn, idx_dim):
    n_heads = d_attn // dh
    group = n_heads // n_kv
    d_kv = n_kv * dh
    nqb = t // QB
    k_top = min(TOPK_MAX, t // 4)
    n_buckets = rel_bias.shape[0]
    assert dh == LANES, "row statistics are kept replicated across one lane tile"
    v_bf = jnp.concatenate([v_bf.reshape(t, n_kv, dh), jnp.ones((t, n_kv, dh), v_bf.dtype)],
                           axis=-1).reshape(t, 2 * d_kv)

    rel = (jnp.arange(2 * QB, dtype=I32)[None, :] - QB) - jnp.arange(QB, dtype=I32)[:, None]
    nb = _bias_lookup(rel_bias, _t5_bucket(rel, n_buckets))
    nb = jnp.moveaxis(nb, -1, 0) - rel_bias[n_buckets // 2 - 1].astype(F32)[:, None, None]
    nb = nb * LOG2E

    s_qb, s_kb, s_kind, s_first = [], [], [], []
    for qb in range(nqb):
        n_far = -(-max(qb - 1, 0) * QB // FKC)
        for kb in range(n_far):
            s_qb.append(qb); s_kb.append(kb); s_kind.append(0); s_first.append(int(kb == 0))
        s_qb.append(qb); s_kb.append(max(n_far - 1, 0)); s_kind.append(1); s_first.append(int(n_far == 0))
    tabs = [jnp.asarray(np.asarray(a, np.int32)) for a in (s_qb, s_kb, s_kind, s_first)]
    n_steps = len(s_qb)

    cq = lay["q"] // d_attn
    ciq = lay["iq"] // (N_IDX_HEADS * idx_dim)
    cikw = lay["ikw"] // LANES
    prev = lambda s, qb, kb, kd, fs: jnp.maximum(qb[s] - 1, 0)

    grid_spec = pltpu.PrefetchScalarGridSpec(
        num_scalar_prefetch=4,
        grid=(n_steps,),
        in_specs=[
            pl.BlockSpec((QB, d_attn), lambda s, qb, kb, kd, fs: (qb[s], cq)),
            pl.BlockSpec((QB, N_IDX_HEADS * idx_dim), lambda s, qb, kb, kd, fs: (qb[s], ciq)),
            pl.BlockSpec((QB, LANES), lambda s, qb, kb, kd, fs: (qb[s], cikw)),
            pl.BlockSpec((nqb, idx_dim, QB), lambda s, qb, kb, kd, fs: (0, 0, 0)),
            pl.BlockSpec((d_kv, FKC), lambda s, qb, kb, kd, fs: (0, kb[s])),
            pl.BlockSpec((FKC, 2 * d_kv), lambda s, qb, kb, kd, fs: (kb[s], 0)),
            pl.BlockSpec((d_kv, QB), lambda s, qb, kb, kd, fs: (0, prev(s, qb, kb, kd, fs))),
            pl.BlockSpec((d_kv, QB), lambda s, qb, kb, kd, fs: (0, qb[s])),
            pl.BlockSpec((QB, 2 * d_kv), lambda s, qb, kb, kd, fs: (prev(s, qb, kb, kd, fs), 0)),
            pl.BlockSpec((QB, 2 * d_kv), lambda s, qb, kb, kd, fs: (qb[s], 0)),
            pl.BlockSpec((n_heads, QB, 2 * QB), lambda s, qb, kb, kd, fs: (0, 0, 0)),
        ],
        out_specs=pl.BlockSpec((QB, d_attn), lambda s, qb, kb, kd, fs: (qb[s], 0)),
        scratch_shapes=[
            pltpu.VMEM((nqb, QB, QB), I32),
            pltpu.VMEM((QB, 1), I32),
            pltpu.VMEM((QB, 1), I32),
            pltpu.VMEM((n_kv, group * QB, dh), BF16),
            pltpu.VMEM((n_kv, group * QB, LANES), F32),
            pltpu.VMEM((n_kv, group * QB, LANES), F32),
            pltpu.VMEM((n_kv, group * QB, dh), F32),
        ],
    )
    kern = functools.partial(_dsa_prompt_kernel, k_top=k_top, n_kv=n_kv, group=group, dh=dh,
                             idx_dim=idx_dim, n_pos_bits=(t - 1).bit_length())
    return pl.pallas_call(
        kern,
        out_shape=jax.ShapeDtypeStruct((t, d_attn), BF16),
        grid_spec=grid_spec,
        compiler_params=_cparams(1, 48),
        name="dsa_prompt",
    )(*tabs, proj, proj, proj, ik_t3, k_t, v_bf, k_t, k_t, v_bf, v_bf, nb)


def _dsa_sample_kernel(q_ref, iq_ref, ikw_ref, cik_ref, ck_ref, cv_ref, nik_ref, nk_ref, nv_ref,
                       bias_ref, o_ref, *, k_top, n_kv, group, dh, idx_dim, past, n_keys):
    s_len = q_ref.shape[0]
    lp = past + nk_ref.shape[0]
    scale = dh ** -0.5
    ik_all = jnp.concatenate([cik_ref[...].astype(BF16), nik_ref[...]], axis=0)
    k_all = jnp.concatenate([ck_ref[...].astype(BF16), nk_ref[...]], axis=0)
    v_all = jnp.concatenate([cv_ref[...].astype(BF16), nv_ref[...]], axis=0)
    iq = iq_ref[...]
    qi = jnp.concatenate(
        [iq[:, h * idx_dim:(h + 1) * idx_dim] for h in range(N_IDX_HEADS)], axis=0).astype(BF16)
    w = ikw_ref[:, idx_dim:idx_dim + N_IDX_HEADS] * ((N_IDX_HEADS ** -0.5) * (idx_dim ** -0.5))
    sc = _index_scores(qi, w, ik_all, N_IDX_HEADS, s_len, keys_major=True)
    row = lax.broadcasted_iota(I32, (s_len, 1), 0)
    limit = jnp.minimum(((past + row) // CHUNK + 1) * CHUNK, n_keys)
    pos = lax.broadcasted_iota(I32, (s_len, lp), 1)
    allowed = pos < limit
    key = _sort_key(jnp.where(allowed, sc, -jnp.inf))

    def count_ge(cand):
        return jnp.sum((key >= cand).astype(I32), axis=1, keepdims=True)

    thr, _ = _kth_largest_key(count_ge, k_top, s_len, lp)

    def count_tie_below(p):
        return jnp.sum(((key == thr) & (pos < p)).astype(I32), axis=1, keepdims=True)

    cut = _tie_cut(count_tie_below, k_top - count_ge(thr + 1), (lp - 1).bit_length(), s_len)
    sel = _select(key, pos, thr, cut) & allowed
    q = q_ref[...]
    outs = []
    for g in range(n_kv):
        qg = jnp.concatenate(
            [q[:, (g * group + r) * dh:(g * group + r + 1) * dh] for r in range(group)],
            axis=0).astype(BF16)
        lg = lax.dot_general(qg, k_all[:, g * dh:(g + 1) * dh], _NT, preferred_element_type=F32) * scale
        lg = lg.reshape(group, s_len, lp) + bias_ref[g * group:(g + 1) * group]
        lm = jnp.where(sel[None], lg, NEG).reshape(group * s_len, lp)
        p = jnp.exp(lm - jnp.max(lm, axis=1, keepdims=True))
        den = jnp.sum(p, axis=1, keepdims=True)
        out_g = jnp.dot(p.astype(BF16), v_all[:, g * dh:(g + 1) * dh], preferred_element_type=F32) / den
        outs += [out_g[r * s_len:(r + 1) * s_len] for r in range(group)]
    o_ref[...] = jnp.concatenate(outs, axis=1).astype(o_ref.dtype)


def _dsa_sample(proj, lay, caches, news, rel_bias, row0, n_seq, s_len, past, n_kv, dh, d_attn,
                idx_dim):
    n_heads = d_attn // dh
    group = n_heads // n_kv
    d_kv = n_kv * dh
    lp = past + LANES
    n_keys = past + s_len
    k_top = min(TOPK_MAX, n_keys // 4)
    rel = jnp.arange(lp, dtype=I32)[None, :] - (past + jnp.arange(s_len, dtype=I32))[:, None]
    bias = jnp.moveaxis(_bias_lookup(rel_bias, _t5_bucket(rel, rel_bias.shape[0])), -1, 0)
    rb = row0 // s_len
    cq = lay["q"] // d_attn
    ciq = lay["iq"] // (N_IDX_HEADS * idx_dim)
    cikw = lay["ikw"] // LANES
    kern = functools.partial(_dsa_sample_kernel, k_top=k_top, n_kv=n_kv, group=group, dh=dh,
                             idx_dim=idx_dim, past=past, n_keys=n_keys)
    return pl.pallas_call(
        kern,
        out_shape=jax.ShapeDtypeStruct((n_seq * s_len, d_attn), BF16),
        grid=(n_seq,),
        in_specs=[
            pl.BlockSpec((s_len, d_attn), lambda b: (rb + b, cq)),
            pl.BlockSpec((s_len, N_IDX_HEADS * idx_dim), lambda b: (rb + b, ciq)),
            pl.BlockSpec((s_len, LANES), lambda b: (rb + b, cikw)),
            pl.BlockSpec((None, past, idx_dim), lambda b: (b, 0, 0)),
            pl.BlockSpec((None, past, d_kv), lambda b: (b, 0, 0)),
            pl.BlockSpec((None, past, d_kv), lambda b: (b, 0, 0)),
            pl.BlockSpec((None, LANES, idx_dim), lambda b: (b, 0, 0)),
            pl.BlockSpec((None, LANES, d_kv), lambda b: (b, 0, 0)),
            pl.BlockSpec((None, LANES, d_kv), lambda b: (b, 0, 0)),
            pl.BlockSpec((n_heads, s_len, lp), lambda b: (0, 0, 0)),
        ],
        out_specs=pl.BlockSpec((s_len, d_attn), lambda b: (b, 0)),
        compiler_params=_cparams(1, 32),
        name="dsa_sample",
    )(proj, proj, proj, *caches, *news, bias)


def _merge_kernel(yc_ref, ya_ref, gc_ref, ga_ref, wc_ref, wa_ref, o_ref, wcb_ref, wab_ref):
    @pl.when(pl.program_id(1) == 0)
    def _():
        wcb_ref[...] = wc_ref[...].astype(BF16)
        wab_ref[...] = wa_ref[...].astype(BF16)

    bc = jnp.dot(yc_ref[...], wcb_ref[...], preferred_element_type=F32)
    ba = jnp.dot(ya_ref[...], wab_ref[...], preferred_element_type=F32)
    merged = jax.nn.sigmoid(gc_ref[...]) * bc + jax.nn.sigmoid(ga_ref[...]) * ba
    o_ref[...] = merged.astype(o_ref.dtype)


def _merge(y_conv, y_attn, proj, lay, w_bc, w_ba):
    m, dc = y_conv.shape
    da = y_attn.shape[1]
    d = w_bc.shape[1]
    tm = _pick_tile(m, 640, 16)
    tn = _pick_tile(d, 512, 128)
    cgc, cga = lay["gc"] // tn, lay["ga"] // tn
    return pl.pallas_call(
        _merge_kernel,
        out_shape=jax.ShapeDtypeStruct((m, d), BF16),
        grid=(d // tn, m // tm),
        in_specs=[pl.BlockSpec((tm, dc), lambda j, i: (i, 0)),
                  pl.BlockSpec((tm, da), lambda j, i: (i, 0)),
                  pl.BlockSpec((tm, tn), lambda j, i: (i, cgc + j)),
                  pl.BlockSpec((tm, tn), lambda j, i: (i, cga + j)),
                  pl.BlockSpec((dc, tn), lambda j, i: (0, j)),
                  pl.BlockSpec((da, tn), lambda j, i: (0, j))],
        out_specs=pl.BlockSpec((tm, tn), lambda j, i: (i, j)),
        scratch_shapes=[pltpu.VMEM((dc, tn), BF16), pltpu.VMEM((da, tn), BF16)],
        compiler_params=_cparams(2, 48),
        name="merge",
    )(y_conv, y_attn, proj, proj, w_bc, w_ba)


def _out_kernel(a_ref, x_ref, w_ref, o_ref, wb_ref):
    @pl.when(pl.program_id(1) == 0)
    def _():
        wb_ref[...] = w_ref[...].astype(BF16)

    o_ref[...] = x_ref[...] + jnp.dot(a_ref[...], wb_ref[...], preferred_element_type=F32)


def _out_proj(merged, x, w_out):
    m, k = merged.shape
    n = w_out.shape[1]
    tm = _pick_tile(m, 640, 16)
    tn = _pick_tile(n, 512, 128)
    return pl.pallas_call(
        _out_kernel,
        out_shape=jax.ShapeDtypeStruct((m, n), F32),
        grid=(n // tn, m // tm),
        in_specs=[pl.BlockSpec((tm, k), lambda j, i: (i, 0)),
                  pl.BlockSpec((tm, tn), lambda j, i: (i, j)),
                  pl.BlockSpec((k, tn), lambda j, i: (0, j))],
        out_specs=pl.BlockSpec((tm, tn), lambda j, i: (i, j)),
        scratch_shapes=[pltpu.VMEM((k, tn), BF16)],
        compiler_params=_cparams(2, 48),
        name="out_proj",
    )(merged, x, w_out)


def _split_bf16(x):
    hi = x.astype(BF16)
    lo = (x - hi.astype(F32)).astype(BF16)
    return hi, lo


def _router_kernel(x_ref, g_ref, rw_ref, rb_ref, h_ref, idx_ref, gate_ref, rank_ref, cnt_ref, carry_ref):
    @pl.when(pl.program_id(0) == 0)
    def _():
        carry_ref[...] = jnp.zeros(carry_ref.shape, F32)

    xf = x_ref[...]
    xf = xf * lax.rsqrt(jnp.mean(xf * xf, axis=-1, keepdims=True) + EPS)
    h = xf * g_ref[...]
    h_ref[...] = h
    h_hi, h_lo = _split_bf16(h)
    w_hi, w_lo = _split_bf16(rw_ref[...])
    logits = (jnp.dot(h_hi, w_hi, preferred_element_type=F32)
              + (jnp.dot(h_hi, w_lo, preferred_element_type=F32)
                 + jnp.dot(h_lo, w_hi, preferred_element_type=F32)))
    logits = logits + rb_ref[...]
    n_e = logits.shape[1]
    lane = lax.broadcasted_iota(I32, logits.shape, 1)
    vals, idxs = [], []
    cur = logits
    for _ in range(EXPERT_TOP_K):
        mx = jnp.max(cur, axis=1, keepdims=True)
        ix = jnp.min(jnp.where(cur == mx, lane, n_e), axis=1, keepdims=True)
        vals.append(mx)
        idxs.append(ix)
        cur = jnp.where(lane == ix, -jnp.inf, cur)
    top = jnp.concatenate(vals, axis=1)
    e = jnp.exp(top - vals[0])
    gate_ref[...] = e / jnp.sum(e, axis=1, keepdims=True)
    idx_ref[...] = jnp.concatenate(idxs, axis=1)

    tm = logits.shape[0]
    hot = [(lane == ix).astype(F32) for ix in idxs]
    total = hot[0]
    for oh in hot[1:]:
        total = total + oh
    ri = lax.broadcasted_iota(I32, (tm, tm), 0)
    ci = lax.broadcasted_iota(I32, (tm, tm), 1)
    tri = (ci < ri).astype(BF16)
    before = jnp.dot(tri, total.astype(BF16), preferred_element_type=F32) + carry_ref[...]
    rank_ref[...] = jnp.concatenate(
        [jnp.sum(oh * before, axis=1, keepdims=True) for oh in hot], axis=1).astype(I32)
    carry_ref[...] = carry_ref[...] + jnp.sum(total, axis=0, keepdims=True)
    cnt_ref[...] = carry_ref[...].astype(I32)


def _router(x1, g_ffn, router_w, router_b):
    m, d = x1.shape
    n_e = router_w.shape[1]
    tm = _pick_tile(m, 256, SUBLANES)
    row_spec = pl.BlockSpec((tm, EXPERT_TOP_K), lambda i: (i, 0))
    return pl.pallas_call(
        _router_kernel,
        out_shape=(jax.ShapeDtypeStruct((m, d), F32),
                   jax.ShapeDtypeStruct((m, EXPERT_TOP_K), I32),
                   jax.ShapeDtypeStruct((m, EXPERT_TOP_K), F32),
                   jax.ShapeDtypeStruct((m, EXPERT_TOP_K), I32),
                   jax.ShapeDtypeStruct((1, n_e), I32)),
        grid=(m // tm,),
        in_specs=[pl.BlockSpec((tm, d), lambda i: (i, 0)),
                  pl.BlockSpec((1, d), lambda i: (0, 0)),
                  pl.BlockSpec((d, n_e), lambda i: (0, 0)),
                  pl.BlockSpec((1, n_e), lambda i: (0, 0))],
        out_specs=(pl.BlockSpec((tm, d), lambda i: (i, 0)), row_spec, row_spec, row_spec,
                   pl.BlockSpec((1, n_e), lambda i: (0, 0))),
        scratch_shapes=[pltpu.VMEM((1, n_e), F32)],
        compiler_params=_cparams(1, 48),
        name="router",
    )(x1, g_ffn.reshape(1, d), router_w, router_b.reshape(1, n_e))


def _row_copy(src_hbm, buf, sem, src_row, dst_row):
    return pltpu.make_async_copy(src_hbm.at[src_row], buf.at[dst_row], sem)


RUN_BLOCKS = 5
GATHER_ROWS = 128


def _up_kernel(re_ref, rc_ref,
               tok_hbm, h_hbm, wg_ref, wl_ref, bg_ref, bl_ref, o_ref,
               xs_ref, stage_ref, tok_smem, wgb_ref, wlb_ref, sem_idx, sem):
    j = pl.program_id(0)
    cnt = rc_ref[j]
    tm = xs_ref.shape[1]
    n_burst = tm // GATHER_ROWS

    def gather_block(r, carry):
        cp = pltpu.make_async_copy(tok_hbm.at[j * RUN_BLOCKS + r], tok_smem, sem_idx)
        cp.start()
        cp.wait()

        def issue(c):
            def body(i, c2):
                _row_copy(h_hbm, stage_ref.at[c % 2], sem.at[c % 2],
                          tok_smem[c * GATHER_ROWS + i], i).start()
                return c2
            lax.fori_loop(0, GATHER_ROWS, body, 0, unroll=8)

        def drain(c):
            def body(i, c2):
                _row_copy(h_hbm, stage_ref.at[c % 2], sem.at[c % 2], 0, i).wait()
                return c2
            lax.fori_loop(0, GATHER_ROWS, body, 0, unroll=8)

        issue(0)
        for c in range(n_burst):
            if c + 1 < n_burst:
                issue(c + 1)
            drain(c)
            xs_ref[r, c * GATHER_ROWS:(c + 1) * GATHER_ROWS, :] = stage_ref[c % 2].astype(BF16)
        return carry

    @pl.when(pl.program_id(1) == 0)
    def _gather():
        lax.fori_loop(0, cnt, gather_block, 0)

    @pl.when(cnt > 0)
    def _cast():
        wgb_ref[...] = wg_ref[...].astype(BF16)
        wlb_ref[...] = wl_ref[...].astype(BF16)

    def compute_block(r, carry):
        x = xs_ref[r]
        hg = jnp.dot(x, wgb_ref[...], preferred_element_type=F32) + bg_ref[...]
        hl = jnp.dot(x, wlb_ref[...], preferred_element_type=F32) + bl_ref[...]
        x_glu = jnp.minimum(hg, SWIGLU_LIMIT)
        x_lin = jnp.clip(hl, -SWIGLU_LIMIT, SWIGLU_LIMIT)
        act = x_glu * jax.nn.sigmoid(SWIGLU_ALPHA * x_glu) * (x_lin + 1.0)
        o_ref[pl.ds(pl.multiple_of(r * tm, tm), tm), :] = act.astype(o_ref.dtype)
        return carry

    def zero_block(r, carry):
        o_ref[pl.ds(pl.multiple_of(r * tm, tm), tm), :] = jnp.zeros((tm, o_ref.shape[1]), o_ref.dtype)
        return carry

    lax.fori_loop(0, cnt, compute_block, 0)
    lax.fori_loop(cnt, RUN_BLOCKS, zero_block, 0)


def _expert_up(h2, row_tok, run_e, run_cnt, w_up, b_up, tm, tf):
    d = h2.shape[1]
    n_e, _, f2 = w_up.shape
    f = f2 // 2
    nf = f // tf
    n_rows = row_tok.shape[0]
    n_runs = run_e.shape[0]
    wtile = lambda half: (lambda j, t, re, rc: (re[j], 0, half * nf + jnp.where(rc[j] > 0, t, nf - 1)))
    grid_spec = pltpu.PrefetchScalarGridSpec(
        num_scalar_prefetch=2,
        grid=(n_runs, nf),
        in_specs=[
            pl.BlockSpec(memory_space=pl.ANY),
            pl.BlockSpec(memory_space=pl.ANY),
            pl.BlockSpec((None, d, tf), wtile(0)),
            pl.BlockSpec((None, d, tf), wtile(1)),
            pl.BlockSpec((None, 1, tf), wtile(0)),
            pl.BlockSpec((None, 1, tf), wtile(1)),
        ],
        out_specs=pl.BlockSpec((RUN_BLOCKS * tm, tf), lambda j, t, re, rc: (j, t)),
        scratch_shapes=[pltpu.VMEM((RUN_BLOCKS, tm, d), BF16),
                        pltpu.VMEM((2, GATHER_ROWS, d), h2.dtype),
                        pltpu.SMEM((tm,), I32),
                        pltpu.VMEM((d, tf), BF16), pltpu.VMEM((d, tf), BF16),
                        pltpu.SemaphoreType.DMA(()),
                        pltpu.SemaphoreType.DMA((2,))],
    )
    return pl.pallas_call(
        _up_kernel,
        out_shape=jax.ShapeDtypeStruct((n_rows, f), BF16),
        grid_spec=grid_spec,
        compiler_params=_cparams(2, 56),
        name="moe_up",
    )(run_e, run_cnt, row_tok.reshape(n_rows // tm, tm), h2, w_up, w_up,
      b_up.reshape(n_e, 1, f2), b_up.reshape(n_e, 1, f2))


def _down_kernel(re_ref, rc_ref, a_hbm, w_ref, b_ref, o_ref, as_ref, wb_ref, sem):
    j = pl.program_id(0)
    cnt = rc_ref[j]
    tm = as_ref.shape[1]

    def block_copy(r):
        row0 = pl.multiple_of((j * RUN_BLOCKS + r) * tm, tm)
        return pltpu.make_async_copy(a_hbm.at[pl.ds(row0, tm)], as_ref.at[r], sem)

    @pl.when(pl.program_id(1) == 0)
    def _load():
        def start(r, c):
            block_copy(r).start()
            return c

        def wait(r, c):
            block_copy(r).wait()
            return c

        lax.fori_loop(0, cnt, start, 0)
        lax.fori_loop(0, cnt, wait, 0)

    @pl.when(cnt > 0)
    def _cast():
        wb_ref[...] = w_ref[...].astype(BF16)

    def compute_block(r, carry):
        y = jnp.dot(as_ref[r], wb_ref[...], preferred_element_type=F32) + b_ref[...]
        o_ref[pl.ds(pl.multiple_of(r * tm, tm), tm), :] = y
        return carry

    def zero_block(r, carry):
        o_ref[pl.ds(pl.multiple_of(r * tm, tm), tm), :] = jnp.zeros((tm, o_ref.shape[1]), o_ref.dtype)
        return carry

    lax.fori_loop(0, cnt, compute_block, 0)
    lax.fori_loop(cnt, RUN_BLOCKS, zero_block, 0)


def _expert_down(act, run_e, run_cnt, w_down, b_down, tm, tn):
    n_rows, f = act.shape
    n_e, _, d = w_down.shape
    nt = d // tn
    n_runs = run_e.shape[0]
    wtile = lambda j, t, re, rc: (re[j], 0, jnp.where(rc[j] > 0, t, nt - 1))
    grid_spec = pltpu.PrefetchScalarGridSpec(
        num_scalar_prefetch=2,
        grid=(n_runs, nt),
        in_specs=[
            pl.BlockSpec(memory_space=pl.ANY),
            pl.BlockSpec((None, f, tn), wtile),
            pl.BlockSpec((None, 1, tn), wtile),
        ],
        out_specs=pl.BlockSpec((RUN_BLOCKS * tm, tn), lambda j, t, re, rc: (j, t)),
        scratch_shapes=[pltpu.VMEM((RUN_BLOCKS, tm, f), BF16),
                        pltpu.VMEM((f, tn), BF16),
                        pltpu.SemaphoreType.DMA(())],
    )
    return pl.pallas_call(
        _down_kernel,
        out_shape=jax.ShapeDtypeStruct((n_rows, d), F32),
        grid_spec=grid_spec,
        compiler_params=_cparams(2, 56),
        name="moe_down",
    )(run_e, run_cnt, act, w_down, b_down.reshape(n_e, 1, d))


def _combine_kernel(pos_hbm, y_hbm, x_ref, gt_ref, g_ref, o_ref, pos_smem, buf, sem_idx, sem):
    i = pl.program_id(0)
    n_steps = pl.num_programs(0)
    tb = x_ref.shape[0]
    n = pos_smem.shape[1]
    cur = i % 2
    nxt = 1 - cur

    def idx_copy(step, sl):
        return pltpu.make_async_copy(pos_hbm.at[step], pos_smem.at[sl], sem_idx.at[sl])

    def issue_rows(sl):
        def body(r, c):
            _row_copy(y_hbm, buf.at[sl], sem.at[sl], pos_smem[sl, r], r).start()
            return c
        lax.fori_loop(0, n, body, 0, unroll=8)

    @pl.when(i == 0)
    def _():
        idx_copy(0, 0).start()
        idx_copy(0, 0).wait()
        issue_rows(0)

        @pl.when(n_steps > 1)
        def _():
            idx_copy(1, 1).start()

    @pl.when(i + 1 < n_steps)
    def _():
        idx_copy(i + 1, nxt).wait()
        issue_rows(nxt)

    @pl.when(i + 2 < n_steps)
    def _():
        idx_copy(i + 2, cur).start()

    def drain(r, c):
        _row_copy(y_hbm, buf.at[cur], sem.at[cur], 0, r).wait()
        return c

    lax.fori_loop(0, n, drain, 0, unroll=8)
    gt = gt_ref[...]
    moe = None
    for j in range(EXPERT_TOP_K):
        term = buf[cur, j * tb:(j + 1) * tb, :] * gt[:, j:j + 1]
        moe = term if moe is None else moe + term
    xf = x_ref[...] + moe
    xf = xf * lax.rsqrt(jnp.mean(xf * xf, axis=-1, keepdims=True) + EPS)
    o_ref[...] = xf * g_ref[...]


def _combine(y_rows, pos, gates, x1, g_final, tb):
    m, d = x1.shape
    nb = m // tb
    n = EXPERT_TOP_K * tb
    pos_blocks = pos.reshape(nb, tb, EXPERT_TOP_K).transpose(0, 2, 1).reshape(nb, n)
    return pl.pallas_call(
        _combine_kernel,
        out_shape=jax.ShapeDtypeStruct((m, d), F32),
        grid=(nb,),
        in_specs=[pl.BlockSpec(memory_space=pl.ANY), pl.BlockSpec(memory_space=pl.ANY),
                  pl.BlockSpec((tb, d), lambda i: (i, 0)),
                  pl.BlockSpec((tb, EXPERT_TOP_K), lambda i: (i, 0)),
                  pl.BlockSpec((1, d), lambda i: (0, 0))],
        out_specs=pl.BlockSpec((tb, d), lambda i: (i, 0)),
        scratch_shapes=[pltpu.SMEM((2, n), I32),
                        pltpu.VMEM((2, n, d), F32),
                        pltpu.SemaphoreType.DMA((2,)),
                        pltpu.SemaphoreType.DMA((2,))],
        compiler_params=_cparams(1, 32),
        name="moe_combine",
    )(pos_blocks, y_rows, x1, gates, g_final.reshape(1, d))


def _routing_tables(top_idx, rank, counts, tm):
    t = top_idx.shape[0]
    n = t * EXPERT_TOP_K
    n_experts = counts.shape[0]
    n_runs = (-(-n // tm) + n_experts) // RUN_BLOCKS + n_experts
    nblk_e = (counts + tm - 1) // tm
    nrun_e = (nblk_e + RUN_BLOCKS - 1) // RUN_BLOCKS
    run_end = jnp.cumsum(nrun_e)
    run_start = run_end - nrun_e
    experts = jnp.arange(n_experts, dtype=I32)
    row0 = run_start * (RUN_BLOCKS * tm)
    hot = top_idx[:, :, None] == experts[None, None, :]
    pos = jnp.sum(jnp.where(hot, row0[None, None, :], 0), axis=-1) + rank
    tok = jnp.broadcast_to(jnp.arange(t, dtype=I32)[:, None], pos.shape)
    row_tok = jnp.zeros((n_runs * RUN_BLOCKS * tm,), I32).at[pos.reshape(n)].set(tok.reshape(n))
    j = jnp.arange(n_runs, dtype=I32)
    e_j = jnp.minimum(jnp.sum((run_end[None, :] <= j[:, None]).astype(I32), axis=1), n_experts - 1)
    sel = e_j[:, None] == experts[None, :]
    pick = lambda a: jnp.sum(jnp.where(sel, a[None, :], 0), axis=1)
    in_e = j - pick(run_start)
    run_cnt = jnp.clip(pick(nblk_e) - in_e * RUN_BLOCKS, 0, RUN_BLOCKS)
    run_cnt = jnp.where(j < run_end[-1], run_cnt, 0).astype(I32)
    last_e = jnp.sum(jnp.where(j == run_end[-1] - 1, e_j, 0))
    run_e = jnp.where(run_cnt > 0, e_j, last_e).astype(I32)
    return pos, row_tok, run_e, run_cnt


def _proj_layout(d, dc, d_attn, d_kv, idx_dim):
    iq_w = N_IDX_HEADS * idx_dim
    assert idx_dim + N_IDX_HEADS <= LANES
    segs = [("gc", d), ("ga", d), ("u", dc), ("c", dc), ("b", dc), ("q", d_attn),
            ("iq", iq_w), ("k", d_kv), ("v", d_kv), ("ikw", LANES)]
    segs.sort(key=lambda s: -s[1])
    lay, off = {}, 0
    for name, width in segs:
        assert off % width == 0, (name, off, width)
        lay[name] = off
        off += width
    lay["total"] = -(-off // 512) * 512
    return lay


def _relayout_w_in(w, lay, dc, d_attn, d_kv, idx_dim, d):
    widths = (dc, dc, dc, d_attn, d_kv, d_kv, N_IDX_HEADS * idx_dim, idx_dim, N_IDX_HEADS, d, d)
    names = ("u", "c", "b", "q", "k", "v", "iq", "ik", "iw", "gc", "ga")
    src, o = {}, 0
    for nme, wd in zip(names, widths):
        src[nme] = (o, wd)
        o += wd
    assert o == w.shape[1]
    out = jnp.zeros((w.shape[0], lay["total"]), w.dtype)
    for nme in ("gc", "ga", "u", "c", "b", "q", "iq", "k", "v"):
        s0, wd = src[nme]
        out = lax.dynamic_update_slice(out, w[:, s0:s0 + wd], (0, lay[nme]))
    s0 = src["ik"][0]
    out = lax.dynamic_update_slice(out, w[:, s0:s0 + idx_dim + N_IDX_HEADS], (0, lay["ikw"]))
    return out


def kernel(x_prompt, x_sample, cache_k, cache_v, cache_idx_k, state_conv, g_mix, w_in, conv_w,
           w_branch_conv, w_branch_attn, w_out, rel_bias, g_ffn, router_w, router_b, w_up, b_up,
           w_down, b_down, g_final):
    depth = w_in.shape[0]
    assert depth == 1, "one trunk layer"
    bp, t, d = x_prompt.shape
    assert bp == 1
    n_seq, s_len, _ = x_sample.shape
    past, n_kv, dh = cache_k.shape[2:5]
    idx_dim = cache_idx_k.shape[-1]
    dc = conv_w.shape[-1]
    d_attn = w_branch_attn.shape[1]
    d_kv = n_kv * dh
    n_experts = router_w.shape[-1]
    assert t % FKC == 0 and s_len % 16 == 0 and s_len <= LANES and past % 16 == 0

    lay = _proj_layout(d, dc, d_attn, d_kv, idx_dim)
    w_in_r = _relayout_w_in(w_in[0], lay, dc, d_attn, d_kv, idx_dim, d)

    x_all = jnp.concatenate([x_prompt.reshape(t, d), x_sample.reshape(n_seq * s_len, d)], axis=0)
    h = _rmsnorm(x_all, g_mix[0], BF16)
    proj = _matmul(h, w_in_r)

    k_new = proj[:, lay["k"]:lay["k"] + d_kv]
    v_new = proj[:, lay["v"]:lay["v"] + d_kv]
    ik_new = proj[:, lay["ikw"]:lay["ikw"] + idx_dim]

    y_conv, st_p, st_s = _short_conv(proj, lay, conv_w[0], state_conv[0], t, n_seq, s_len)

    k_t = k_new[:t].T.astype(BF16)
    v_bf = v_new[:t].astype(BF16)
    ik_t3 = ik_new[:t].reshape(t // QB, QB, idx_dim).transpose(0, 2, 1).astype(BF16)
    ya_p = _dsa_prompt(proj, lay, ik_t3, k_t, v_bf, rel_bias, t, n_kv, dh, d_attn, idx_dim)

    def new_rows(new, width):
        rows = new[t:].reshape(n_seq, s_len, width).astype(BF16)
        return jnp.pad(rows, ((0, 0), (0, LANES - s_len), (0, 0)))

    caches = (cache_idx_k[0], cache_k[0].reshape(n_seq, past, d_kv), cache_v[0].reshape(n_seq, past, d_kv))
    news = (new_rows(ik_new, idx_dim), new_rows(k_new, d_kv), new_rows(v_new, d_kv))
    ya_s = _dsa_sample(proj, lay, caches, news, rel_bias,
                       t, n_seq, s_len, past, n_kv, dh, d_attn, idx_dim)
    y_attn = jnp.concatenate([ya_p, ya_s], axis=0)

    merged = _merge(y_conv, y_attn, proj, lay, w_branch_conv[0], w_branch_attn[0])
    x1 = _out_proj(merged, x_all, w_out[0])

    h2, top_idx, gates, rank, counts = _router(x1, g_ffn[0], router_w[0], router_b[0])
    tm_e = 512 if (x_all.shape[0] * EXPERT_TOP_K) >= 16 * 512 else 128
    pos, row_tok, run_e, run_cnt = _routing_tables(top_idx, rank, counts.reshape(n_experts), tm_e)
    tf = _pick_tile(w_down.shape[2], 256, LANES)
    tn = _pick_tile(d, 512, LANES)
    act = _expert_up(h2, row_tok, run_e, run_cnt, w_up[0], b_up[0], tm_e, tf)
    y_rows = _expert_down(act, run_e, run_cnt, w_down[0], b_down[0], tm_e, tn)
    tb = _pick_tile(x_all.shape[0], 64, SUBLANES)
    y_all = _combine(y_rows, pos, gates, x1, g_final, tb)

    y_prompt = y_all[:t].reshape(1, t, d)
    y_sample = y_all[t:].reshape(n_seq, s_len, d)
    return (y_prompt, y_sample,
            k_new[:t].reshape(1, 1, t, n_kv, dh), v_new[:t].reshape(1, 1, t, n_kv, dh),
            ik_new[:t].reshape(1, 1, t, idx_dim), st_p.reshape(1, 1, 2, dc),
            k_new[t:].reshape(1, n_seq, s_len, n_kv, dh), v_new[t:].reshape(1, n_seq, s_len, n_kv, dh),
            ik_new[t:].reshape(1, n_seq, s_len, idx_dim), st_s.reshape(1, n_seq, 2, dc))
```

```python
import functools
import math

import numpy as np
import jax
import jax.numpy as jnp
from jax import lax
from jax.experimental import pallas as pl
from jax.experimental.pallas import tpu as pltpu

F32 = jnp.float32
BF16 = jnp.bfloat16
I32 = jnp.int32

CHUNK = 64
EPS = 1e-6
TOPK_MAX = 256
N_IDX_HEADS = 16
MAX_DISTANCE = 128
EXPERT_TOP_K = 4
SWIGLU_LIMIT = 7.0
SWIGLU_ALPHA = 1.702

LANES = 128
SUBLANES = 8
VMEM_BYTES_V7X = 64 * 1024 * 1024

QB = 128
KC = 512
SUB = KC // QB
FKC = 1024
FSUB = FKC // QB
SLAB = 32
LOG2E = 1.4426950408889634
NEG = -1e30
INT_MIN = -(2 ** 31)
INT_MAX = 2 ** 31 - 1


def _cparams(n_axes, vmem_mb):
    return pltpu.CompilerParams(
        dimension_semantics=("arbitrary",) * n_axes,
        vmem_limit_bytes=vmem_mb * 1024 * 1024,
    )


def _pick_tile(n, max_tile, quantum):
    best = None
    t = quantum
    while t <= min(n, max_tile):
        if n % t == 0:
            best = t
        t += quantum
    assert best is not None, (n, max_tile, quantum)
    return best


def _rmsnorm_kernel(xp_ref, xs_ref, g_ref, h_ref, x_ref, *, n_prompt_blocks):
    xf = jnp.where(pl.program_id(0) < n_prompt_blocks, xp_ref[...], xs_ref[...])
    x_ref[...] = xf
    xf = xf * lax.rsqrt(jnp.mean(xf * xf, axis=-1, keepdims=True) + EPS)
    h_ref[...] = (xf * g_ref[...]).astype(h_ref.dtype)


def _rmsnorm_rows(xp, xs, g):
    tp, d = xp.shape
    ts = xs.shape[0]
    tm = _pick_tile(math.gcd(tp, ts), 256, SUBLANES * 2)
    nbp = tp // tm
    row_spec = pl.BlockSpec((tm, d), lambda i: (i, 0))
    return pl.pallas_call(
        functools.partial(_rmsnorm_kernel, n_prompt_blocks=nbp),
        out_shape=(jax.ShapeDtypeStruct((tp + ts, d), BF16), jax.ShapeDtypeStruct((tp + ts, d), F32)),
        grid=((tp + ts) // tm,),
        in_specs=[pl.BlockSpec((tm, d), lambda i: (jnp.minimum(i, nbp - 1), 0)),
                  pl.BlockSpec((tm, d), lambda i: (jnp.maximum(i - nbp, 0), 0)),
                  pl.BlockSpec((1, d), lambda i: (0, 0))],
        out_specs=(row_spec, row_spec),
        compiler_params=_cparams(1, 32),
        name="rmsnorm",
    )(xp, xs, g.reshape(1, d))


def _mm_kernel(a_ref, w_ref, o_ref, wb_ref):
    @pl.when(pl.program_id(1) == 0)
    def _():
        wb_ref[...] = w_ref[...].astype(BF16)

    o_ref[...] = jnp.dot(a_ref[...], wb_ref[...], preferred_element_type=F32)


def _matmul(a, w, n_cols=None):
    m, k = a.shape
    n = w.shape[1] if n_cols is None else n_cols
    tm = _pick_tile(m, 1280, 16)
    tn = _pick_tile(n, 512, 128)
    return pl.pallas_call(
        _mm_kernel,
        out_shape=jax.ShapeDtypeStruct((m, n), F32),
        grid=(n // tn, m // tm),
        in_specs=[pl.BlockSpec((tm, k), lambda j, i: (i, 0)),
                  pl.BlockSpec((k, tn), lambda j, i: (0, j))],
        out_specs=pl.BlockSpec((tm, tn), lambda j, i: (i, j)),
        scratch_shapes=[pltpu.VMEM((k, tn), BF16)],
        compiler_params=_cparams(2, 56),
        name="in_proj",
    )(a, w)


def _conv_prompt_kernel(u_ref, c_ref, b_ref, up_ref, cp_ref, w_ref, y_ref, st_ref):
    i = pl.program_id(0)
    tm = u_ref.shape[0]
    v = c_ref[...] * u_ref[...]
    hv = cp_ref[...] * up_ref[...]
    hv = jnp.where(i > 0, hv, 0.0)
    vp = jnp.concatenate([hv, v], axis=0)
    w = w_ref[...]
    h = SUBLANES
    y = w[0:1] * vp[h - 2:h - 2 + tm] + w[1:2] * vp[h - 1:h - 1 + tm]
    y = y + w[2:3] * v
    y_ref[...] = (b_ref[...] * y).astype(y_ref.dtype)

    @pl.when(i == pl.num_programs(0) - 1)
    def _():
        st_ref[...] = v[tm - 2:tm]


def _conv_sample_kernel(u_ref, c_ref, b_ref, s_ref, w_ref, y_ref, st_ref):
    t = u_ref.shape[0]
    v = c_ref[...] * u_ref[...]
    vp = jnp.concatenate([s_ref[...], v], axis=0)
    w = w_ref[...]
    y = w[0:1] * vp[0:t] + w[1:2] * vp[1:t + 1]
    y = y + w[2:3] * vp[2:t + 2]
    y_ref[...] = (b_ref[...] * y).astype(y_ref.dtype)
    st_ref[...] = vp[t:t + 2]


def _short_conv(proj, lay, conv_w, state, t_prompt, n_seq, s_len):
    dc = conv_w.shape[1]
    cu, cc, cb = lay["u"] // dc, lay["c"] // dc, lay["b"] // dc
    tm = _pick_tile(t_prompt, 256, 16)
    hb = tm // SUBLANES
    y_p, st_p = pl.pallas_call(
        _conv_prompt_kernel,
        out_shape=(jax.ShapeDtypeStruct((t_prompt, dc), BF16),
                   jax.ShapeDtypeStruct((2, dc), F32)),
        grid=(t_prompt // tm,),
        in_specs=[pl.BlockSpec((tm, dc), lambda i: (i, cu)),
                  pl.BlockSpec((tm, dc), lambda i: (i, cc)),
                  pl.BlockSpec((tm, dc), lambda i: (i, cb)),
                  pl.BlockSpec((SUBLANES, dc), lambda i: (jnp.maximum(i * hb - 1, 0), cu)),
                  pl.BlockSpec((SUBLANES, dc), lambda i: (jnp.maximum(i * hb - 1, 0), cc)),
                  pl.BlockSpec((3, dc), lambda i: (0, 0))],
        out_specs=(pl.BlockSpec((tm, dc), lambda i: (i, 0)),
                   pl.BlockSpec((2, dc), lambda i: (0, 0))),
        compiler_params=_cparams(1, 32),
        name="conv_prompt",
    )(proj, proj, proj, proj, proj, conv_w)

    rb = t_prompt // s_len
    y_s, st_s = pl.pallas_call(
        _conv_sample_kernel,
        out_shape=(jax.ShapeDtypeStruct((n_seq * s_len, dc), BF16),
                   jax.ShapeDtypeStruct((n_seq, 2, dc), F32)),
        grid=(n_seq,),
        in_specs=[pl.BlockSpec((s_len, dc), lambda b: (rb + b, cu)),
                  pl.BlockSpec((s_len, dc), lambda b: (rb + b, cc)),
                  pl.BlockSpec((s_len, dc), lambda b: (rb + b, cb)),
                  pl.BlockSpec((None, 2, dc), lambda b: (b, 0, 0)),
                  pl.BlockSpec((3, dc), lambda b: (0, 0))],
        out_specs=(pl.BlockSpec((s_len, dc), lambda b: (b, 0)),
                   pl.BlockSpec((None, 2, dc), lambda b: (b, 0, 0))),
        compiler_params=_cparams(1, 32),
        name="conv_sample",
    )(proj, proj, proj, state, conv_w)
    return jnp.concatenate([y_p, y_s], axis=0), st_p, st_s


def _sort_key(score):
    score = jnp.where(score == 0.0, 0.0, score)
    bits = lax.bitcast_convert_type(score, I32)
    return jnp.where(bits < 0, bits ^ 0x7FFFFFFF, bits)


def _kth_largest_key(count_ge, k_top, rows, n_keys):
    zero = jnp.zeros((rows, 1), I32)
    c0 = count_ge(zero)
    pos_side = c0 >= k_top
    base = jnp.where(pos_side, zero, jnp.full((rows, 1), INT_MIN, I32))
    cb = jnp.where(pos_side, c0, n_keys)

    def unsettled(cb):
        return jnp.max((cb != k_top).astype(I32))

    def cond(c):
        it, _, _, todo = c
        return (it < 31) & (todo > 0)

    def body(c):
        it, base, cb, _ = c
        cand = base | (jnp.int32(1) << (30 - it))
        cnt = count_ge(cand)
        take = cnt >= k_top
        cb = jnp.where(take, cnt, cb)
        return it + 1, jnp.where(take, cand, base), cb, unsettled(cb)

    _, base, cb, _ = lax.while_loop(cond, body, (jnp.int32(0), base, cb, unsettled(cb)))
    return base, cb


def _tie_cut(count_tie_below, need, n_bits, rows):
    def body(it, p):
        cand = p | (jnp.int32(1) << (n_bits - 1 - it))
        return jnp.where(count_tie_below(cand) < need, cand, p)

    return lax.fori_loop(0, n_bits, body, jnp.zeros((rows, 1), I32))


def _select(key, pos, thr, cut):
    return (key > thr) | ((key == thr) & (pos <= cut))


_NT = (((1,), (1,)), ((), ()))


def _index_scores(qi, w, ikc, n_heads, rows, keys_major=False):
    if keys_major:
        s_all = lax.dot_general(qi, ikc, _NT, preferred_element_type=F32)
    else:
        s_all = jnp.dot(qi, ikc, preferred_element_type=F32)
    acc = None
    for h in range(n_heads):
        term = w[:, h:h + 1] * jnp.maximum(s_all[h * rows:(h + 1) * rows], 0.0)
        acc = term if acc is None else acc + term
    return acc


def _lane_fold(x, op):
    out = x[:, 0:LANES]
    for j in range(1, x.shape[1] // LANES):
        out = op(out, x[:, j * LANES:(j + 1) * LANES])
    return out


def _softmax_step(lg, madd, bias_slab, v_g, m_ref, l_ref, acc_ref, g):
    rows_g = lg.shape[0]
    dh = acc_ref.shape[-1]

    def logits(r0):
        q_off = r0 % QB
        x = lg[r0:r0 + SLAB] + madd[q_off:q_off + SLAB]
        b = bias_slab(r0 // QB, q_off)
        return x if b is None else x + b

    part = jnp.concatenate([_lane_fold(logits(r0), jnp.maximum) for r0 in range(0, rows_g, SLAB)], axis=0)
    m_old = m_ref[g]
    m_new = jnp.maximum(m_old, jnp.broadcast_to(jnp.max(part, axis=1, keepdims=True), m_old.shape))
    alpha = jnp.exp2(m_old - m_new)
    ps = []
    for r0 in range(0, rows_g, SLAB):
        x = logits(r0)
        m_s = m_new[r0:r0 + SLAB]
        ps.append(jnp.concatenate(
            [jnp.exp2(x[:, j * LANES:(j + 1) * LANES] - m_s) for j in range(x.shape[1] // LANES)],
            axis=1).astype(BF16))
    pv = jnp.dot(jnp.concatenate(ps, axis=0), v_g, preferred_element_type=F32)
    m_ref[g] = m_new
    l_ref[g] = alpha * l_ref[g] + pv[:, dh:]
    acc_ref[g] = alpha * acc_ref[g] + pv[:, :dh]


def _dsa_prompt_kernel(qb_ref, kb_ref, kind_ref, first_ref,
                       q_ref, iq_ref, ikw_ref, ikt_ref, ktf_ref, vf_ref,
                       kta_ref, ktb_ref, va_ref, vb_ref, nb_ref,
                       o_ref,
                       sc_ref, thr_ref, cut_ref, qs_ref, m_ref, l_ref, acc_ref,
                       *, k_top, n_kv, group, dh, idx_dim, n_pos_bits):
    s = pl.program_id(0)
    qb = qb_ref[s]
    kb = kb_ref[s]
    q0 = qb * QB
    coef = (dh ** -0.5) * LOG2E
    row = lax.broadcasted_iota(I32, (QB, 1), 0)
    limit = q0 + (row // CHUNK + 1) * CHUNK

    @pl.when(first_ref[s] == 1)
    def _setup():
        iq = iq_ref[...]
        qi = jnp.concatenate(
            [iq[:, h * idx_dim:(h + 1) * idx_dim] for h in range(N_IDX_HEADS)], axis=0).astype(BF16)
        w = ikw_ref[:, idx_dim:idx_dim + N_IDX_HEADS] * ((N_IDX_HEADS ** -0.5) * (idx_dim ** -0.5))
        n_sw = qb // SUB + 1

        def score_body(c, carry):
            ikc = jnp.concatenate([ikt_ref[c * SUB + j] for j in range(SUB)], axis=1)
            sc = _index_scores(qi, w, ikc, N_IDX_HEADS, QB)
            pos = c * KC + lax.broadcasted_iota(I32, (QB, KC), 1)
            key = _sort_key(jnp.where(pos < limit, sc, -jnp.inf))
            for j in range(SUB):
                sc_ref[c * SUB + j] = key[:, j * QB:(j + 1) * QB]
            return carry

        lax.fori_loop(0, n_sw, score_body, 0)

        def count_ge(cand):
            def body(c, acc):
                for j in range(SUB):
                    acc = acc + (sc_ref[c * SUB + j] >= cand).astype(I32)
                return acc
            acc = lax.fori_loop(0, n_sw, body, jnp.zeros((QB, QB), I32))
            return jnp.sum(acc, axis=1, keepdims=True)

        thr_k, cnt_k = _kth_largest_key(count_ge, k_top, QB, n_sw * KC)
        thr_ref[...] = thr_k
        cut_ref[...] = jnp.full((QB, 1), INT_MAX, I32)
        excess = cnt_k > k_top

        @pl.when(jnp.max(excess.astype(I32)) > 0)
        def _ties():
            def count_tie_below(p):
                def body(c, acc):
                    for j in range(SUB):
                        pos = (c * SUB + j) * QB + lax.broadcasted_iota(I32, (QB, QB), 1)
                        acc = acc + ((sc_ref[c * SUB + j] == thr_k) & (pos < p)).astype(I32)
                    return acc
                acc = lax.fori_loop(0, n_sw, body, jnp.zeros((QB, QB), I32))
                return jnp.sum(acc, axis=1, keepdims=True)

            need = k_top - count_ge(thr_k + 1)
            cut = _tie_cut(count_tie_below, need, n_pos_bits, QB)
            cut_ref[...] = jnp.where(excess, cut, INT_MAX)

        q = q_ref[...] * coef
        for g in range(n_kv):
            qs_ref[g] = jnp.concatenate(
                [q[:, (g * group + r) * dh:(g * group + r + 1) * dh] for r in range(group)],
                axis=0).astype(BF16)
        m_ref[...] = jnp.full(m_ref.shape, NEG, F32)
        l_ref[...] = jnp.zeros(l_ref.shape, F32)
        acc_ref[...] = jnp.zeros(acc_ref.shape, F32)

    thr = thr_ref[...]
    cut = cut_ref[...]

    @pl.when(kind_ref[s] == 0)
    def _far():
        pos = kb * FKC + lax.broadcasted_iota(I32, (QB, FKC), 1)
        key = jnp.concatenate([sc_ref[jnp.minimum(kb * FSUB + j, qb)] for j in range(FSUB)], axis=1)
        sel = _select(key, pos, thr, cut)
        sel = sel & (pos < q0 - QB)
        madd = jnp.where(sel, 0.0, NEG)
        lgs = [jnp.dot(qs_ref[g], ktf_ref[g * dh:(g + 1) * dh, :], preferred_element_type=F32)
               for g in range(n_kv)]
        for g in range(n_kv):
            _softmax_step(lgs[g], madd, lambda hh, q_off: None,
                          vf_ref[:, g * 2 * dh:(g + 1) * 2 * dh], m_ref, l_ref, acc_ref, g)

    @pl.when(kind_ref[s] == 1)
    def _window():
        w_keys = 2 * QB
        pos = q0 - QB + lax.broadcasted_iota(I32, (QB, w_keys), 1)
        key = jnp.concatenate([sc_ref[jnp.maximum(qb - 1, 0)], sc_ref[qb]], axis=1)
        sel = _select(key, pos, thr, cut) & (pos >= 0) & (pos < limit)
        kt = jnp.concatenate([kta_ref[...], ktb_ref[...]], axis=1)
        v = jnp.concatenate([va_ref[...], vb_ref[...]], axis=0)
        madd = jnp.where(sel, 0.0, NEG)
        outs = []
        for g in range(n_kv):
            lg = jnp.dot(qs_ref[g], kt[g * dh:(g + 1) * dh, :], preferred_element_type=F32)
            _softmax_step(lg, madd,
                          lambda hh, q_off, g=g: nb_ref[g * group + hh, q_off:q_off + SLAB, :],
                          v[:, g * 2 * dh:(g + 1) * 2 * dh], m_ref, l_ref, acc_ref, g)
            out_g = acc_ref[g] / l_ref[g]
            outs += [out_g[r * QB:(r + 1) * QB] for r in range(group)]
        o_ref[...] = jnp.concatenate(outs, axis=1).astype(o_ref.dtype)


def _t5_bucket(rel, n_buckets):
    nb = n_buckets // 2
    max_exact = nb // 2
    n = jnp.abs(rel)
    nf = jnp.maximum(n, 1).astype(F32)
    large = max_exact + (jnp.log(nf / max_exact) / math.log(MAX_DISTANCE / max_exact)
                         * (nb - max_exact)).astype(I32)
    large = jnp.minimum(large, nb - 1)
    return jnp.where(rel > 0, nb, 0) + jnp.where(n < max_exact, n, large)


def _bias_lookup(rel_bias, bucket):
    out = jnp.zeros(bucket.shape + (rel_bias.shape[1],), F32)
    for b in range(rel_bias.shape[0]):
        out = jnp.where((bucket == b)[..., None], rel_bias[b].astype(F32), out)
    return out


def _dsa_prompt(proj, lay, ik_t3, k_t, v_bf, rel_bias, t, n_kv, dh, d_attn, idx_dim):
    n_heads = d_attn // dh
    group = n_heads // n_kv
    d_kv = n_kv * dh
    nqb = t // QB
    k_top = min(TOPK_MAX, t // 4)
    n_buckets = rel_bias.shape[0]
    assert dh == LANES, "row statistics are kept replicated across one lane tile"
    v_bf = jnp.concatenate([v_bf.reshape(t, n_kv, dh), jnp.ones((t, n_kv, dh), v_bf.dtype)],
                           axis=-1).reshape(t, 2 * d_kv)

    rel = (jnp.arange(2 * QB, dtype=I32)[None, :] - QB) - jnp.arange(QB, dtype=I32)[:, None]
    nb = _bias_lookup(rel_bias, _t5_bucket(rel, n_buckets))
    nb = jnp.moveaxis(nb, -1, 0) - rel_bias[n_buckets // 2 - 1].astype(F32)[:, None, None]
    nb = nb * LOG2E

    s_qb, s_kb, s_kind, s_first = [], [], [], []
    for qb in range(nqb):
        n_far = -(-max(qb - 1, 0) * QB // FKC)
        for kb in range(n_far):
            s_qb.append(qb); s_kb.append(kb); s_kind.append(0); s_first.append(int(kb == 0))
        s_qb.append(qb); s_kb.append(max(n_far - 1, 0)); s_kind.append(1); s_first.append(int(n_far == 0))
    tabs = [jnp.asarray(np.asarray(a, np.int32)) for a in (s_qb, s_kb, s_kind, s_first)]
    n_steps = len(s_qb)

    cq = lay["q"] // d_attn
    ciq = lay["iq"] // (N_IDX_HEADS * idx_dim)
    cikw = lay["ikw"] // LANES
    prev = lambda s, qb, kb, kd, fs: jnp.maximum(qb[s] - 1, 0)

    grid_spec = pltpu.PrefetchScalarGridSpec(
        num_scalar_prefetch=4,
        grid=(n_steps,),
        in_specs=[
            pl.BlockSpec((QB, d_attn), lambda s, qb, kb, kd, fs: (qb[s], cq)),
            pl.BlockSpec((QB, N_IDX_HEADS * idx_dim), lambda s, qb, kb, kd, fs: (qb[s], ciq)),
            pl.BlockSpec((QB, LANES), lambda s, qb, kb, kd, fs: (qb[s], cikw)),
            pl.BlockSpec((nqb, idx_dim, QB), lambda s, qb, kb, kd, fs: (0, 0, 0)),
            pl.BlockSpec((d_kv, FKC), lambda s, qb, kb, kd, fs: (0, kb[s])),
            pl.BlockSpec((FKC, 2 * d_kv), lambda s, qb, kb, kd, fs: (kb[s], 0)),
            pl.BlockSpec((d_kv, QB), lambda s, qb, kb, kd, fs: (0, prev(s, qb, kb, kd, fs))),
            pl.BlockSpec((d_kv, QB), lambda s, qb, kb, kd, fs: (0, qb[s])),
            pl.BlockSpec((QB, 2 * d_kv), lambda s, qb, kb, kd, fs: (prev(s, qb, kb, kd, fs), 0)),
            pl.BlockSpec((QB, 2 * d_kv), lambda s, qb, kb, kd, fs: (qb[s], 0)),
            pl.BlockSpec((n_heads, QB, 2 * QB), lambda s, qb, kb, kd, fs: (0, 0, 0)),
        ],
        out_specs=pl.BlockSpec((QB, d_attn), lambda s, qb, kb, kd, fs: (qb[s], 0)),
        scratch_shapes=[
            pltpu.VMEM((nqb, QB, QB), I32),
            pltpu.VMEM((QB, 1), I32),
            pltpu.VMEM((QB, 1), I32),
            pltpu.VMEM((n_kv, group * QB, dh), BF16),
            pltpu.VMEM((n_kv, group * QB, LANES), F32),
            pltpu.VMEM((n_kv, group * QB, LANES), F32),
            pltpu.VMEM((n_kv, group * QB, dh), F32),
        ],
    )
    kern = functools.partial(_dsa_prompt_kernel, k_top=k_top, n_kv=n_kv, group=group, dh=dh,
                             idx_dim=idx_dim, n_pos_bits=(t - 1).bit_length())
    return pl.pallas_call(
        kern,
        out_shape=jax.ShapeDtypeStruct((t, d_attn), BF16),
        grid_spec=grid_spec,
        compiler_params=_cparams(1, 48),
        name="dsa_prompt",
    )(*tabs, proj, proj, proj, ik_t3, k_t, v_bf, k_t, k_t, v_bf, v_bf, nb)


def _dsa_sample_kernel(q_ref, iq_ref, ikw_ref, cik_ref, ck_ref, cv_ref, nik_ref, nk_ref, nv_ref,
                       bias_ref, o_ref, *, k_top, n_kv, group, dh, idx_dim, past, n_keys):
    s_len = q_ref.shape[0]
    lp = past + nk_ref.shape[0]
    scale = dh ** -0.5
    ik_all = jnp.concatenate([cik_ref[...].astype(BF16), nik_ref[...]], axis=0)
    k_all = jnp.concatenate([ck_ref[...].astype(BF16), nk_ref[...]], axis=0)
    v_all = jnp.concatenate([cv_ref[...].astype(BF16), nv_ref[...]], axis=0)
    iq = iq_ref[...]
    qi = jnp.concatenate(
        [iq[:, h * idx_dim:(h + 1) * idx_dim] for h in range(N_IDX_HEADS)], axis=0).astype(BF16)
    w = ikw_ref[:, idx_dim:idx_dim + N_IDX_HEADS] * ((N_IDX_HEADS ** -0.5) * (idx_dim ** -0.5))
    sc = _index_scores(qi, w, ik_all, N_IDX_HEADS, s_len, keys_major=True)
    row = lax.broadcasted_iota(I32, (s_len, 1), 0)
    limit = jnp.minimum(((past + row) // CHUNK + 1) * CHUNK, n_keys)
    pos = lax.broadcasted_iota(I32, (s_len, lp), 1)
    allowed = pos < limit
    key = _sort_key(jnp.where(allowed, sc, -jnp.inf))

    def count_ge(cand):
        return jnp.sum((key >= cand).astype(I32), axis=1, keepdims=True)

    thr, _ = _kth_largest_key(count_ge, k_top, s_len, lp)

    def count_tie_below(p):
        return jnp.sum(((key == thr) & (pos < p)).astype(I32), axis=1, keepdims=True)

    cut = _tie_cut(count_tie_below, k_top - count_ge(thr + 1), (lp - 1).bit_length(), s_len)
    sel = _select(key, pos, thr, cut) & allowed
    q = q_ref[...]
    outs = []
    for g in range(n_kv):
        qg = jnp.concatenate(
            [q[:, (g * group + r) * dh:(g * group + r + 1) * dh] for r in range(group)],
            axis=0).astype(BF16)
        lg = lax.dot_general(qg, k_all[:, g * dh:(g + 1) * dh], _NT, preferred_element_type=F32) * scale
        lg = lg.reshape(group, s_len, lp) + bias_ref[g * group:(g + 1) * group]
        lm = jnp.where(sel[None], lg, NEG).reshape(group * s_len, lp)
        p = jnp.exp(lm - jnp.max(lm, axis=1, keepdims=True))
        den = jnp.sum(p, axis=1, keepdims=True)
        out_g = jnp.dot(p.astype(BF16), v_all[:, g * dh:(g + 1) * dh], preferred_element_type=F32) / den
        outs += [out_g[r * s_len:(r + 1) * s_len] for r in range(group)]
    o_ref[...] = jnp.concatenate(outs, axis=1).astype(o_ref.dtype)


def _dsa_sample(proj, lay, caches, news, rel_bias, row0, n_seq, s_len, past, n_kv, dh, d_attn,
                idx_dim):
    n_heads = d_attn // dh
    group = n_heads // n_kv
    d_kv = n_kv * dh
    lp = past + LANES
    n_keys = past + s_len
    k_top = min(TOPK_MAX, n_keys // 4)
    rel = jnp.arange(lp, dtype=I32)[None, :] - (past + jnp.arange(s_len, dtype=I32))[:, None]
    bias = jnp.moveaxis(_bias_lookup(rel_bias, _t5_bucket(rel, rel_bias.shape[0])), -1, 0)
    rb = row0 // s_len
    cq = lay["q"] // d_attn
    ciq = lay["iq"] // (N_IDX_HEADS * idx_dim)
    cikw = lay["ikw"] // LANES
    kern = functools.partial(_dsa_sample_kernel, k_top=k_top, n_kv=n_kv, group=group, dh=dh,
                             idx_dim=idx_dim, past=past, n_keys=n_keys)
    return pl.pallas_call(
        kern,
        out_shape=jax.ShapeDtypeStruct((n_seq * s_len, d_attn), BF16),
        grid=(n_seq,),
        in_specs=[
            pl.BlockSpec((s_len, d_attn), lambda b: (rb + b, cq)),
            pl.BlockSpec((s_len, N_IDX_HEADS * idx_dim), lambda b: (rb + b, ciq)),
            pl.BlockSpec((s_len, LANES), lambda b: (rb + b, cikw)),
            pl.BlockSpec((None, past, idx_dim), lambda b: (b, 0, 0)),
            pl.BlockSpec((None, past, d_kv), lambda b: (b, 0, 0)),
            pl.BlockSpec((None, past, d_kv), lambda b: (b, 0, 0)),
            pl.BlockSpec((None, LANES, idx_dim), lambda b: (b, 0, 0)),
            pl.BlockSpec((None, LANES, d_kv), lambda b: (b, 0, 0)),
            pl.BlockSpec((None, LANES, d_kv), lambda b: (b, 0, 0)),
            pl.BlockSpec((n_heads, s_len, lp), lambda b: (0, 0, 0)),
        ],
        out_specs=pl.BlockSpec((s_len, d_attn), lambda b: (b, 0)),
        compiler_params=_cparams(1, 32),
        name="dsa_sample",
    )(proj, proj, proj, *caches, *news, bias)


def _merge_kernel(yc_ref, ya_ref, gc_ref, ga_ref, wc_ref, wa_ref, o_ref, wcb_ref, wab_ref):
    @pl.when(pl.program_id(1) == 0)
    def _():
        wcb_ref[...] = wc_ref[...].astype(BF16)
        wab_ref[...] = wa_ref[...].astype(BF16)

    bc = jnp.dot(yc_ref[...], wcb_ref[...], preferred_element_type=F32)
    ba = jnp.dot(ya_ref[...], wab_ref[...], preferred_element_type=F32)
    merged = jax.nn.sigmoid(gc_ref[...]) * bc + jax.nn.sigmoid(ga_ref[...]) * ba
    o_ref[...] = merged.astype(o_ref.dtype)


def _merge(y_conv, y_attn, gates_proj, w_bc, w_ba):
    m, dc = y_conv.shape
    da = y_attn.shape[1]
    d = w_bc.shape[1]
    tm = _pick_tile(m, 640, 16)
    tn = _pick_tile(d, 512, 128)
    cgc, cga = 0, d // tn
    return pl.pallas_call(
        _merge_kernel,
        out_shape=jax.ShapeDtypeStruct((m, d), BF16),
        grid=(d // tn, m // tm),
        in_specs=[pl.BlockSpec((tm, dc), lambda j, i: (i, 0)),
                  pl.BlockSpec((tm, da), lambda j, i: (i, 0)),
                  pl.BlockSpec((tm, tn), lambda j, i: (i, cgc + j)),
                  pl.BlockSpec((tm, tn), lambda j, i: (i, cga + j)),
                  pl.BlockSpec((dc, tn), lambda j, i: (0, j)),
                  pl.BlockSpec((da, tn), lambda j, i: (0, j))],
        out_specs=pl.BlockSpec((tm, tn), lambda j, i: (i, j)),
        scratch_shapes=[pltpu.VMEM((dc, tn), BF16), pltpu.VMEM((da, tn), BF16)],
        compiler_params=_cparams(2, 48),
        name="merge",
    )(y_conv, y_attn, gates_proj, gates_proj, w_bc, w_ba)


def _out_kernel(a_ref, x_ref, w_ref, o_ref, wb_ref):
    @pl.when(pl.program_id(1) == 0)
    def _():
        wb_ref[...] = w_ref[...].astype(BF16)

    o_ref[...] = x_ref[...] + jnp.dot(a_ref[...], wb_ref[...], preferred_element_type=F32)


def _out_proj(merged, x, w_out):
    m, k = merged.shape
    n = w_out.shape[1]
    tm = _pick_tile(m, 640, 16)
    tn = _pick_tile(n, 512, 128)
    return pl.pallas_call(
        _out_kernel,
        out_shape=jax.ShapeDtypeStruct((m, n), F32),
        grid=(n // tn, m // tm),
        in_specs=[pl.BlockSpec((tm, k), lambda j, i: (i, 0)),
                  pl.BlockSpec((tm, tn), lambda j, i: (i, j)),
                  pl.BlockSpec((k, tn), lambda j, i: (0, j))],
        out_specs=pl.BlockSpec((tm, tn), lambda j, i: (i, j)),
        scratch_shapes=[pltpu.VMEM((k, tn), BF16)],
        compiler_params=_cparams(2, 48),
        name="out_proj",
    )(merged, x, w_out)


def _split_bf16(x):
    hi = x.astype(BF16)
    lo = (x - hi.astype(F32)).astype(BF16)
    return hi, lo


def _router_kernel(x_ref, g_ref, rw_ref, rb_ref, h_ref, idx_ref, gate_ref, rank_ref, cnt_ref, carry_ref):
    @pl.when(pl.program_id(0) == 0)
    def _():
        carry_ref[...] = jnp.zeros(carry_ref.shape, F32)

    xf = x_ref[...]
    xf = xf * lax.rsqrt(jnp.mean(xf * xf, axis=-1, keepdims=True) + EPS)
    h = xf * g_ref[...]
    h_ref[...] = h
    h_hi, h_lo = _split_bf16(h)
    w_hi, w_lo = _split_bf16(rw_ref[...])
    logits = (jnp.dot(h_hi, w_hi, preferred_element_type=F32)
              + (jnp.dot(h_hi, w_lo, preferred_element_type=F32)
                 + jnp.dot(h_lo, w_hi, preferred_element_type=F32)))
    logits = logits + rb_ref[...]
    n_e = logits.shape[1]
    lane = lax.broadcasted_iota(I32, logits.shape, 1)
    vals, idxs = [], []
    cur = logits
    for _ in range(EXPERT_TOP_K):
        mx = jnp.max(cur, axis=1, keepdims=True)
        ix = jnp.min(jnp.where(cur == mx, lane, n_e), axis=1, keepdims=True)
        vals.append(mx)
        idxs.append(ix)
        cur = jnp.where(lane == ix, -jnp.inf, cur)
    top = jnp.concatenate(vals, axis=1)
    e = jnp.exp(top - vals[0])
    gate_ref[...] = e / jnp.sum(e, axis=1, keepdims=True)
    idx_ref[...] = jnp.concatenate(idxs, axis=1)

    tm = logits.shape[0]
    hot = [(lane == ix).astype(F32) for ix in idxs]
    total = hot[0]
    for oh in hot[1:]:
        total = total + oh
    ri = lax.broadcasted_iota(I32, (tm, tm), 0)
    ci = lax.broadcasted_iota(I32, (tm, tm), 1)
    tri = (ci < ri).astype(BF16)
    before = jnp.dot(tri, total.astype(BF16), preferred_element_type=F32) + carry_ref[...]
    rank_ref[...] = jnp.concatenate(
        [jnp.sum(oh * before, axis=1, keepdims=True) for oh in hot], axis=1).astype(I32)
    carry_ref[...] = carry_ref[...] + jnp.sum(total, axis=0, keepdims=True)
    cnt_ref[...] = carry_ref[...].astype(I32)


def _router(x1, g_ffn, router_w, router_b):
    m, d = x1.shape
    n_e = router_w.shape[1]
    tm = _pick_tile(m, 256, SUBLANES)
    row_spec = pl.BlockSpec((tm, EXPERT_TOP_K), lambda i: (i, 0))
    return pl.pallas_call(
        _router_kernel,
        out_shape=(jax.ShapeDtypeStruct((m, d), F32),
                   jax.ShapeDtypeStruct((m, EXPERT_TOP_K), I32),
                   jax.ShapeDtypeStruct((m, EXPERT_TOP_K), F32),
                   jax.ShapeDtypeStruct((m, EXPERT_TOP_K), I32),
                   jax.ShapeDtypeStruct((1, n_e), I32)),
        grid=(m // tm,),
        in_specs=[pl.BlockSpec((tm, d), lambda i: (i, 0)),
                  pl.BlockSpec((1, d), lambda i: (0, 0)),
                  pl.BlockSpec((d, n_e), lambda i: (0, 0)),
                  pl.BlockSpec((1, n_e), lambda i: (0, 0))],
        out_specs=(pl.BlockSpec((tm, d), lambda i: (i, 0)), row_spec, row_spec, row_spec,
                   pl.BlockSpec((1, n_e), lambda i: (0, 0))),
        scratch_shapes=[pltpu.VMEM((1, n_e), F32)],
        compiler_params=_cparams(1, 48),
        name="router",
    )(x1, g_ffn.reshape(1, d), router_w, router_b.reshape(1, n_e))


def _row_copy(src_hbm, buf, sem, src_row, dst_row):
    return pltpu.make_async_copy(src_hbm.at[src_row], buf.at[dst_row], sem)


RUN_BLOCKS = 5
GATHER_ROWS = 128


def _up_kernel(re_ref, rc_ref,
               tok_hbm, h_hbm, wg_ref, wl_ref, bg_ref, bl_ref, o_ref,
               xs_ref, stage_ref, tok_smem, wgb_ref, wlb_ref, sem_idx, sem):
    j = pl.program_id(0)
    cnt = rc_ref[j]
    tm = xs_ref.shape[1]
    n_burst = tm // GATHER_ROWS

    def gather_block(r, carry):
        cp = pltpu.make_async_copy(tok_hbm.at[j * RUN_BLOCKS + r], tok_smem, sem_idx)
        cp.start()
        cp.wait()

        def issue(c):
            def body(i, c2):
                _row_copy(h_hbm, stage_ref.at[c % 2], sem.at[c % 2],
                          tok_smem[c * GATHER_ROWS + i], i).start()
                return c2
            lax.fori_loop(0, GATHER_ROWS, body, 0, unroll=8)

        def drain(c):
            def body(i, c2):
                _row_copy(h_hbm, stage_ref.at[c % 2], sem.at[c % 2], 0, i).wait()
                return c2
            lax.fori_loop(0, GATHER_ROWS, body, 0, unroll=8)

        issue(0)
        for c in range(n_burst):
            if c + 1 < n_burst:
                issue(c + 1)
            drain(c)
            xs_ref[r, c * GATHER_ROWS:(c + 1) * GATHER_ROWS, :] = stage_ref[c % 2].astype(BF16)
        return carry

    @pl.when(pl.program_id(1) == 0)
    def _gather():
        lax.fori_loop(0, cnt, gather_block, 0)

    @pl.when(cnt > 0)
    def _cast():
        wgb_ref[...] = wg_ref[...].astype(BF16)
        wlb_ref[...] = wl_ref[...].astype(BF16)

    def compute_block(r, carry):
        x = xs_ref[r]
        hg = jnp.dot(x, wgb_ref[...], preferred_element_type=F32) + bg_ref[...]
        hl = jnp.dot(x, wlb_ref[...], preferred_element_type=F32) + bl_ref[...]
        x_glu = jnp.minimum(hg, SWIGLU_LIMIT)
        x_lin = jnp.clip(hl, -SWIGLU_LIMIT, SWIGLU_LIMIT)
        act = x_glu * jax.nn.sigmoid(SWIGLU_ALPHA * x_glu) * (x_lin + 1.0)
        o_ref[pl.ds(pl.multiple_of(r * tm, tm), tm), :] = act.astype(o_ref.dtype)
        return carry

    def zero_block(r, carry):
        o_ref[pl.ds(pl.multiple_of(r * tm, tm), tm), :] = jnp.zeros((tm, o_ref.shape[1]), o_ref.dtype)
        return carry

    lax.fori_loop(0, cnt, compute_block, 0)
    lax.fori_loop(cnt, RUN_BLOCKS, zero_block, 0)


def _expert_up(h2, row_tok, run_e, run_cnt, w_up, b_up, tm, tf):
    d = h2.shape[1]
    n_e, _, f2 = w_up.shape
    f = f2 // 2
    nf = f // tf
    n_rows = row_tok.shape[0]
    n_runs = run_e.shape[0]
    wtile = lambda half: (lambda j, t, re, rc: (re[j], 0, half * nf + jnp.where(rc[j] > 0, t, nf - 1)))
    grid_spec = pltpu.PrefetchScalarGridSpec(
        num_scalar_prefetch=2,
        grid=(n_runs, nf),
        in_specs=[
            pl.BlockSpec(memory_space=pl.ANY),
            pl.BlockSpec(memory_space=pl.ANY),
            pl.BlockSpec((None, d, tf), wtile(0)),
            pl.BlockSpec((None, d, tf), wtile(1)),
            pl.BlockSpec((None, 1, tf), wtile(0)),
            pl.BlockSpec((None, 1, tf), wtile(1)),
        ],
        out_specs=pl.BlockSpec((RUN_BLOCKS * tm, tf), lambda j, t, re, rc: (j, t)),
        scratch_shapes=[pltpu.VMEM((RUN_BLOCKS, tm, d), BF16),
                        pltpu.VMEM((2, GATHER_ROWS, d), h2.dtype),
                        pltpu.SMEM((tm,), I32),
                        pltpu.VMEM((d, tf), BF16), pltpu.VMEM((d, tf), BF16),
                        pltpu.SemaphoreType.DMA(()),
                        pltpu.SemaphoreType.DMA((2,))],
    )
    return pl.pallas_call(
        _up_kernel,
        out_shape=jax.ShapeDtypeStruct((n_rows, f), BF16),
        grid_spec=grid_spec,
        compiler_params=_cparams(2, 56),
        name="moe_up",
    )(run_e, run_cnt, row_tok.reshape(n_rows // tm, tm), h2, w_up, w_up,
      b_up.reshape(n_e, 1, f2), b_up.reshape(n_e, 1, f2))


def _down_kernel(re_ref, rc_ref, a_hbm, w_ref, b_ref, o_ref, as_ref, wb_ref, sem):
    j = pl.program_id(0)
    cnt = rc_ref[j]
    tm = as_ref.shape[1]

    def block_copy(r):
        row0 = pl.multiple_of((j * RUN_BLOCKS + r) * tm, tm)
        return pltpu.make_async_copy(a_hbm.at[pl.ds(row0, tm)], as_ref.at[r], sem)

    @pl.when(pl.program_id(1) == 0)
    def _load():
        def start(r, c):
            block_copy(r).start()
            return c

        def wait(r, c):
            block_copy(r).wait()
            return c

        lax.fori_loop(0, cnt, start, 0)
        lax.fori_loop(0, cnt, wait, 0)

    @pl.when(cnt > 0)
    def _cast():
        wb_ref[...] = w_ref[...].astype(BF16)

    def compute_block(r, carry):
        y = jnp.dot(as_ref[r], wb_ref[...], preferred_element_type=F32) + b_ref[...]
        o_ref[pl.ds(pl.multiple_of(r * tm, tm), tm), :] = y
        return carry

    def zero_block(r, carry):
        o_ref[pl.ds(pl.multiple_of(r * tm, tm), tm), :] = jnp.zeros((tm, o_ref.shape[1]), o_ref.dtype)
        return carry

    lax.fori_loop(0, cnt, compute_block, 0)
    lax.fori_loop(cnt, RUN_BLOCKS, zero_block, 0)


def _expert_down(act, run_e, run_cnt, w_down, b_down, tm, tn):
    n_rows, f = act.shape
    n_e, _, d = w_down.shape
    nt = d // tn
    n_runs = run_e.shape[0]
    wtile = lambda j, t, re, rc: (re[j], 0, jnp.where(rc[j] > 0, t, nt - 1))
    grid_spec = pltpu.PrefetchScalarGridSpec(
        num_scalar_prefetch=2,
        grid=(n_runs, nt),
        in_specs=[
            pl.BlockSpec(memory_space=pl.ANY),
            pl.BlockSpec((None, f, tn), wtile),
            pl.BlockSpec((None, 1, tn), wtile),
        ],
        out_specs=pl.BlockSpec((RUN_BLOCKS * tm, tn), lambda j, t, re, rc: (j, t)),
        scratch_shapes=[pltpu.VMEM((RUN_BLOCKS, tm, f), BF16),
                        pltpu.VMEM((f, tn), BF16),
                        pltpu.SemaphoreType.DMA(())],
    )
    return pl.pallas_call(
        _down_kernel,
        out_shape=jax.ShapeDtypeStruct((n_rows, d), F32),
        grid_spec=grid_spec,
        compiler_params=_cparams(2, 56),
        name="moe_down",
    )(run_e, run_cnt, act, w_down, b_down.reshape(n_e, 1, d))


def _combine_kernel(pos_hbm, y_hbm, x_ref, gt_ref, g_ref, o_ref, pos_smem, buf, sem_idx, sem):
    i = pl.program_id(0)
    n_steps = pl.num_programs(0)
    tb = x_ref.shape[0]
    n = pos_smem.shape[1]
    cur = i % 2
    nxt = 1 - cur

    def idx_copy(step, sl):
        return pltpu.make_async_copy(pos_hbm.at[step], pos_smem.at[sl], sem_idx.at[sl])

    def issue_rows(sl):
        def body(r, c):
            _row_copy(y_hbm, buf.at[sl], sem.at[sl], pos_smem[sl, r], r).start()
            return c
        lax.fori_loop(0, n, body, 0, unroll=8)

    @pl.when(i == 0)
    def _():
        idx_copy(0, 0).start()
        idx_copy(0, 0).wait()
        issue_rows(0)

        @pl.when(n_steps > 1)
        def _():
            idx_copy(1, 1).start()

    @pl.when(i + 1 < n_steps)
    def _():
        idx_copy(i + 1, nxt).wait()
        issue_rows(nxt)

    @pl.when(i + 2 < n_steps)
    def _():
        idx_copy(i + 2, cur).start()

    def drain(r, c):
        _row_copy(y_hbm, buf.at[cur], sem.at[cur], 0, r).wait()
        return c

    lax.fori_loop(0, n, drain, 0, unroll=8)
    gt = gt_ref[...]
    moe = None
    for j in range(EXPERT_TOP_K):
        term = buf[cur, j * tb:(j + 1) * tb, :] * gt[:, j:j + 1]
        moe = term if moe is None else moe + term
    xf = x_ref[...] + moe
    xf = xf * lax.rsqrt(jnp.mean(xf * xf, axis=-1, keepdims=True) + EPS)
    o_ref[...] = xf * g_ref[...]


def _combine(y_rows, pos, gates, x1, g_final, tb):
    m, d = x1.shape
    nb = m // tb
    n = EXPERT_TOP_K * tb
    pos_blocks = pos.reshape(nb, tb, EXPERT_TOP_K).transpose(0, 2, 1).reshape(nb, n)
    return pl.pallas_call(
        _combine_kernel,
        out_shape=jax.ShapeDtypeStruct((m, d), F32),
        grid=(nb,),
        in_specs=[pl.BlockSpec(memory_space=pl.ANY), pl.BlockSpec(memory_space=pl.ANY),
                  pl.BlockSpec((tb, d), lambda i: (i, 0)),
                  pl.BlockSpec((tb, EXPERT_TOP_K), lambda i: (i, 0)),
                  pl.BlockSpec((1, d), lambda i: (0, 0))],
        out_specs=pl.BlockSpec((tb, d), lambda i: (i, 0)),
        scratch_shapes=[pltpu.SMEM((2, n), I32),
                        pltpu.VMEM((2, n, d), F32),
                        pltpu.SemaphoreType.DMA((2,)),
                        pltpu.SemaphoreType.DMA((2,))],
        compiler_params=_cparams(1, 32),
        name="moe_combine",
    )(pos_blocks, y_rows, x1, gates, g_final.reshape(1, d))


def _routing_tables(top_idx, rank, counts, tm):
    t = top_idx.shape[0]
    n = t * EXPERT_TOP_K
    n_experts = counts.shape[0]
    n_runs = (-(-n // tm) + n_experts) // RUN_BLOCKS + n_experts
    nblk_e = (counts + tm - 1) // tm
    nrun_e = (nblk_e + RUN_BLOCKS - 1) // RUN_BLOCKS
    run_end = jnp.cumsum(nrun_e)
    run_start = run_end - nrun_e
    experts = jnp.arange(n_experts, dtype=I32)
    row0 = run_start * (RUN_BLOCKS * tm)
    hot = top_idx[:, :, None] == experts[None, None, :]
    pos = jnp.sum(jnp.where(hot, row0[None, None, :], 0), axis=-1) + rank
    tok = jnp.broadcast_to(jnp.arange(t, dtype=I32)[:, None], pos.shape)
    row_tok = jnp.zeros((n_runs * RUN_BLOCKS * tm,), I32).at[pos.reshape(n)].set(tok.reshape(n))
    j = jnp.arange(n_runs, dtype=I32)
    e_j = jnp.minimum(jnp.sum((run_end[None, :] <= j[:, None]).astype(I32), axis=1), n_experts - 1)
    sel = e_j[:, None] == experts[None, :]
    pick = lambda a: jnp.sum(jnp.where(sel, a[None, :], 0), axis=1)
    in_e = j - pick(run_start)
    run_cnt = jnp.clip(pick(nblk_e) - in_e * RUN_BLOCKS, 0, RUN_BLOCKS)
    run_cnt = jnp.where(j < run_end[-1], run_cnt, 0).astype(I32)
    last_e = jnp.sum(jnp.where(j == run_end[-1] - 1, e_j, 0))
    run_e = jnp.where(run_cnt > 0, e_j, last_e).astype(I32)
    return pos, row_tok, run_e, run_cnt


def _proj_layout(d, dc, d_attn, d_kv, idx_dim):
    iq_w = N_IDX_HEADS * idx_dim
    assert idx_dim + N_IDX_HEADS <= LANES
    lay, off = {}, 0
    for name, width in (("u", dc), ("c", dc), ("b", dc), ("q", d_attn), ("k", d_kv), ("v", d_kv),
                        ("iq", iq_w), ("ikw", idx_dim + N_IDX_HEADS)):
        lay[name] = off
        off += width
    for name, width in (("u", dc), ("c", dc), ("b", dc), ("q", d_attn), ("iq", iq_w), ("ikw", LANES)):
        assert lay[name] % width == 0, (name, lay[name], width)
    lay["gate_src"] = off
    lay["main_cols"] = -(-(lay["ikw"] + LANES) // 512) * 512
    assert lay["main_cols"] <= off + 2 * d
    return lay


def kernel(x_prompt, x_sample, cache_k, cache_v, cache_idx_k, state_conv, g_mix, w_in, conv_w,
           w_branch_conv, w_branch_attn, w_out, rel_bias, g_ffn, router_w, router_b, w_up, b_up,
           w_down, b_down, g_final):
    depth = w_in.shape[0]
    assert depth == 1, "one trunk layer"
    bp, t, d = x_prompt.shape
    assert bp == 1
    n_seq, s_len, _ = x_sample.shape
    past, n_kv, dh = cache_k.shape[2:5]
    idx_dim = cache_idx_k.shape[-1]
    dc = conv_w.shape[-1]
    d_attn = w_branch_attn.shape[1]
    d_kv = n_kv * dh
    n_experts = router_w.shape[-1]
    assert t % FKC == 0 and s_len % 16 == 0 and s_len <= LANES and past % 16 == 0

    lay = _proj_layout(d, dc, d_attn, d_kv, idx_dim)
    assert lay["gate_src"] + 2 * d == w_in.shape[2]

    h, x_all = _rmsnorm_rows(x_prompt.reshape(t, d), x_sample.reshape(n_seq * s_len, d), g_mix[0])
    proj = _matmul(h, w_in[0], n_cols=lay["main_cols"])
    gates_proj = _matmul(h, w_in[0][:, lay["gate_src"]:])

    k_new = proj[:, lay["k"]:lay["k"] + d_kv]
    v_new = proj[:, lay["v"]:lay["v"] + d_kv]
    ik_new = proj[:, lay["ikw"]:lay["ikw"] + idx_dim]

    y_conv, st_p, st_s = _short_conv(proj, lay, conv_w[0], state_conv[0], t, n_seq, s_len)

    k_t = k_new[:t].T.astype(BF16)
    v_bf = v_new[:t].astype(BF16)
    ik_t3 = ik_new[:t].reshape(t // QB, QB, idx_dim).transpose(0, 2, 1).astype(BF16)
    ya_p = _dsa_prompt(proj, lay, ik_t3, k_t, v_bf, rel_bias, t, n_kv, dh, d_attn, idx_dim)

    def new_rows(new, width):
        rows = new[t:].reshape(n_seq, s_len, width).astype(BF16)
        return jnp.pad(rows, ((0, 0), (0, LANES - s_len), (0, 0)))

    caches = (cache_idx_k[0], cache_k[0].reshape(n_seq, past, d_kv), cache_v[0].reshape(n_seq, past, d_kv))
    news = (new_rows(ik_new, idx_dim), new_rows(k_new, d_kv), new_rows(v_new, d_kv))
    ya_s = _dsa_sample(proj, lay, caches, news, rel_bias,
                       t, n_seq, s_len, past, n_kv, dh, d_attn, idx_dim)
    y_attn = jnp.concatenate([ya_p, ya_s], axis=0)

    merged = _merge(y_conv, y_attn, gates_proj, w_branch_conv[0], w_branch_attn[0])
    x1 = _out_proj(merged, x_all, w_out[0])

    h2, top_idx, gates, rank, counts = _router(x1, g_ffn[0], router_w[0], router_b[0])
    tm_e = 512 if (x_all.shape[0] * EXPERT_TOP_K) >= 16 * 512 else 128
    pos, row_tok, run_e, run_cnt = _routing_tables(top_idx, rank, counts.reshape(n_experts), tm_e)
    tf = _pick_tile(w_down.shape[2], 256, LANES)
    tn = _pick_tile(d, 512, LANES)
    act = _expert_up(h2, row_tok, run_e, run_cnt, w_up[0], b_up[0], tm_e, tf)
    y_rows = _expert_down(act, run_e, run_cnt, w_down[0], b_down[0], tm_e, tn)
    tb = _pick_tile(x_all.shape[0], 64, SUBLANES)
    y_all = _combine(y_rows, pos, gates, x1, g_final, tb)

    y_prompt = y_all[:t].reshape(1, t, d)
    y_sample = y_all[t:].reshape(n_seq, s_len, d)
    return (y_prompt, y_sample,
            k_new[:t].reshape(1, 1, t, n_kv, dh), v_new[:t].reshape(1, 1, t, n_kv, dh),
            ik_new[:t].reshape(1, 1, t, idx_dim), st_p.reshape(1, 1, 2, dc),
            k_new[t:].reshape(1, n_seq, s_len, n_kv, dh), v_new[t:].reshape(1, n_seq, s_len, n_kv, dh),
            ik_new[t:].reshape(1, n_seq, s_len, idx_dim), st_s.reshape(1, n_seq, 2, dc))
```

```python
import functools
import math

import numpy as np
import jax
import jax.numpy as jnp
from jax import lax
from jax.experimental import pallas as pl
from jax.experimental.pallas import tpu as pltpu

F32 = jnp.float32
BF16 = jnp.bfloat16
I32 = jnp.int32

CHUNK = 64
EPS = 1e-6
TOPK_MAX = 256
N_IDX_HEADS = 16
MAX_DISTANCE = 128
EXPERT_TOP_K = 4
SWIGLU_LIMIT = 7.0
SWIGLU_ALPHA = 1.702

LANES = 128
SUBLANES = 8
VMEM_BYTES_V7X = 64 * 1024 * 1024

QB = 128
KC = 512
SUB = KC // QB
FKC = 1024
FSUB = FKC // QB
SLAB = 32
LOG2E = 1.4426950408889634
NEG = -1e30
INT_MIN = -(2 ** 31)
INT_MAX = 2 ** 31 - 1


def _cparams(n_axes, vmem_mb):
    return pltpu.CompilerParams(
        dimension_semantics=("arbitrary",) * n_axes,
        vmem_limit_bytes=vmem_mb * 1024 * 1024,
    )


def _pick_tile(n, max_tile, quantum):
    best = None
    t = quantum
    while t <= min(n, max_tile):
        if n % t == 0:
            best = t
        t += quantum
    assert best is not None, (n, max_tile, quantum)
    return best


def _rmsnorm_kernel(xp_ref, xs_ref, g_ref, h_ref, x_ref, *, n_prompt_blocks):
    xf = jnp.where(pl.program_id(0) < n_prompt_blocks, xp_ref[...], xs_ref[...])
    x_ref[...] = xf
    xf = xf * lax.rsqrt(jnp.mean(xf * xf, axis=-1, keepdims=True) + EPS)
    h_ref[...] = (xf * g_ref[...]).astype(h_ref.dtype)


def _rmsnorm_rows(xp, xs, g):
    tp, d = xp.shape
    ts = xs.shape[0]
    tm = _pick_tile(math.gcd(tp, ts), 256, SUBLANES * 2)
    nbp = tp // tm
    row_spec = pl.BlockSpec((tm, d), lambda i: (i, 0))
    return pl.pallas_call(
        functools.partial(_rmsnorm_kernel, n_prompt_blocks=nbp),
        out_shape=(jax.ShapeDtypeStruct((tp + ts, d), BF16), jax.ShapeDtypeStruct((tp + ts, d), F32)),
        grid=((tp + ts) // tm,),
        in_specs=[pl.BlockSpec((tm, d), lambda i: (jnp.minimum(i, nbp - 1), 0)),
                  pl.BlockSpec((tm, d), lambda i: (jnp.maximum(i - nbp, 0), 0)),
                  pl.BlockSpec((1, d), lambda i: (0, 0))],
        out_specs=(row_spec, row_spec),
        compiler_params=_cparams(1, 32),
        name="rmsnorm",
    )(xp, xs, g.reshape(1, d))


def _mm_kernel(a_ref, w_ref, o_ref, wb_ref):
    @pl.when(pl.program_id(1) == 0)
    def _():
        wb_ref[...] = w_ref[...].astype(BF16)

    o_ref[...] = jnp.dot(a_ref[...], wb_ref[...], preferred_element_type=F32)


def _matmul(a, w, n_cols=None):
    m, k = a.shape
    n = w.shape[1] if n_cols is None else n_cols
    tm = _pick_tile(m, 1280, 16)
    tn = _pick_tile(n, 512, 128)
    return pl.pallas_call(
        _mm_kernel,
        out_shape=jax.ShapeDtypeStruct((m, n), F32),
        grid=(n // tn, m // tm),
        in_specs=[pl.BlockSpec((tm, k), lambda j, i: (i, 0)),
                  pl.BlockSpec((k, tn), lambda j, i: (0, j))],
        out_specs=pl.BlockSpec((tm, tn), lambda j, i: (i, j)),
        scratch_shapes=[pltpu.VMEM((k, tn), BF16)],
        compiler_params=_cparams(2, 56),
        name="in_proj",
    )(a, w)


def _conv_prompt_kernel(u_ref, c_ref, b_ref, up_ref, cp_ref, w_ref, y_ref, st_ref):
    i = pl.program_id(0)
    tm = u_ref.shape[0]
    v = c_ref[...] * u_ref[...]
    hv = cp_ref[...] * up_ref[...]
    hv = jnp.where(i > 0, hv, 0.0)
    vp = jnp.concatenate([hv, v], axis=0)
    w = w_ref[...]
    h = SUBLANES
    y = w[0:1] * vp[h - 2:h - 2 + tm] + w[1:2] * vp[h - 1:h - 1 + tm]
    y = y + w[2:3] * v
    y_ref[...] = (b_ref[...] * y).astype(y_ref.dtype)

    @pl.when(i == pl.num_programs(0) - 1)
    def _():
        st_ref[...] = v[tm - 2:tm]


def _conv_sample_kernel(u_ref, c_ref, b_ref, s_ref, w_ref, y_ref, st_ref):
    t = u_ref.shape[0]
    v = c_ref[...] * u_ref[...]
    vp = jnp.concatenate([s_ref[...], v], axis=0)
    w = w_ref[...]
    y = w[0:1] * vp[0:t] + w[1:2] * vp[1:t + 1]
    y = y + w[2:3] * vp[2:t + 2]
    y_ref[...] = (b_ref[...] * y).astype(y_ref.dtype)
    st_ref[...] = vp[t:t + 2]


def _short_conv(proj, lay, conv_w, state, t_prompt, n_seq, s_len):
    dc = conv_w.shape[1]
    cu, cc, cb = lay["u"] // dc, lay["c"] // dc, lay["b"] // dc
    tm = _pick_tile(t_prompt, 256, 16)
    hb = tm // SUBLANES
    y_p, st_p = pl.pallas_call(
        _conv_prompt_kernel,
        out_shape=(jax.ShapeDtypeStruct((t_prompt, dc), BF16),
                   jax.ShapeDtypeStruct((2, dc), F32)),
        grid=(t_prompt // tm,),
        in_specs=[pl.BlockSpec((tm, dc), lambda i: (i, cu)),
                  pl.BlockSpec((tm, dc), lambda i: (i, cc)),
                  pl.BlockSpec((tm, dc), lambda i: (i, cb)),
                  pl.BlockSpec((SUBLANES, dc), lambda i: (jnp.maximum(i * hb - 1, 0), cu)),
                  pl.BlockSpec((SUBLANES, dc), lambda i: (jnp.maximum(i * hb - 1, 0), cc)),
                  pl.BlockSpec((3, dc), lambda i: (0, 0))],
        out_specs=(pl.BlockSpec((tm, dc), lambda i: (i, 0)),
                   pl.BlockSpec((2, dc), lambda i: (0, 0))),
        compiler_params=_cparams(1, 32),
        name="conv_prompt",
    )(proj, proj, proj, proj, proj, conv_w)

    rb = t_prompt // s_len
    y_s, st_s = pl.pallas_call(
        _conv_sample_kernel,
        out_shape=(jax.ShapeDtypeStruct((n_seq * s_len, dc), BF16),
                   jax.ShapeDtypeStruct((n_seq, 2, dc), F32)),
        grid=(n_seq,),
        in_specs=[pl.BlockSpec((s_len, dc), lambda b: (rb + b, cu)),
                  pl.BlockSpec((s_len, dc), lambda b: (rb + b, cc)),
                  pl.BlockSpec((s_len, dc), lambda b: (rb + b, cb)),
                  pl.BlockSpec((None, 2, dc), lambda b: (b, 0, 0)),
                  pl.BlockSpec((3, dc), lambda b: (0, 0))],
        out_specs=(pl.BlockSpec((s_len, dc), lambda b: (b, 0)),
                   pl.BlockSpec((None, 2, dc), lambda b: (b, 0, 0))),
        compiler_params=_cparams(1, 32),
        name="conv_sample",
    )(proj, proj, proj, state, conv_w)
    return jnp.concatenate([y_p, y_s], axis=0), st_p, st_s


def _sort_key(score):
    score = jnp.where(score == 0.0, 0.0, score)
    bits = lax.bitcast_convert_type(score, I32)
    return jnp.where(bits < 0, bits ^ 0x7FFFFFFF, bits)


def _kth_largest_key(count_ge, k_top, rows, n_keys):
    zero = jnp.zeros((rows, 1), I32)
    c0 = count_ge(zero)
    pos_side = c0 >= k_top
    base = jnp.where(pos_side, zero, jnp.full((rows, 1), INT_MIN, I32))
    cb = jnp.where(pos_side, c0, n_keys)

    def unsettled(cb):
        return jnp.max((cb != k_top).astype(I32))

    def cond(c):
        it, _, _, todo = c
        return (it < 31) & (todo > 0)

    def body(c):
        it, base, cb, _ = c
        cand = base | (jnp.int32(1) << (30 - it))
        cnt = count_ge(cand)
        take = cnt >= k_top
        cb = jnp.where(take, cnt, cb)
        return it + 1, jnp.where(take, cand, base), cb, unsettled(cb)

    _, base, cb, _ = lax.while_loop(cond, body, (jnp.int32(0), base, cb, unsettled(cb)))
    return base, cb


def _tie_cut(count_tie_below, need, n_bits, rows):
    def body(it, p):
        cand = p | (jnp.int32(1) << (n_bits - 1 - it))
        return jnp.where(count_tie_below(cand) < need, cand, p)

    return lax.fori_loop(0, n_bits, body, jnp.zeros((rows, 1), I32))


def _select(key, pos, thr, cut):
    return (key > thr) | ((key == thr) & (pos <= cut))


_NT = (((1,), (1,)), ((), ()))


def _index_scores(qi, w, ikc, n_heads, rows, keys_major=False):
    if keys_major:
        s_all = lax.dot_general(qi, ikc, _NT, preferred_element_type=F32)
    else:
        s_all = jnp.dot(qi, ikc, preferred_element_type=F32)
    acc = None
    for h in range(n_heads):
        term = w[:, h:h + 1] * jnp.maximum(s_all[h * rows:(h + 1) * rows], 0.0)
        acc = term if acc is None else acc + term
    return acc


def _lane_fold(x, op):
    out = x[:, 0:LANES]
    for j in range(1, x.shape[1] // LANES):
        out = op(out, x[:, j * LANES:(j + 1) * LANES])
    return out


def _softmax_step(lg, madd, bias_slab, v_g, m_ref, l_ref, acc_ref, g):
    rows_g = lg.shape[0]
    dh = acc_ref.shape[-1]

    def logits(r0):
        q_off = r0 % QB
        x = lg[r0:r0 + SLAB] + madd[q_off:q_off + SLAB]
        b = bias_slab(r0 // QB, q_off)
        return x if b is None else x + b

    part = jnp.concatenate([_lane_fold(logits(r0), jnp.maximum) for r0 in range(0, rows_g, SLAB)], axis=0)
    m_old = m_ref[g]
    m_new = jnp.maximum(m_old, jnp.broadcast_to(jnp.max(part, axis=1, keepdims=True), m_old.shape))
    alpha = jnp.exp2(m_old - m_new)
    ps = []
    for r0 in range(0, rows_g, SLAB):
        x = logits(r0)
        m_s = m_new[r0:r0 + SLAB]
        ps.append(jnp.concatenate(
            [jnp.exp2(x[:, j * LANES:(j + 1) * LANES] - m_s) for j in range(x.shape[1] // LANES)],
            axis=1).astype(BF16))
    pv = jnp.dot(jnp.concatenate(ps, axis=0), v_g, preferred_element_type=F32)
    m_ref[g] = m_new
    l_ref[g] = alpha * l_ref[g] + pv[:, dh:]
    acc_ref[g] = alpha * acc_ref[g] + pv[:, :dh]


def _dsa_prompt_kernel(qb_ref, kb_ref, kind_ref, first_ref,
                       q_ref, iq_ref, ikw_ref, ikt_ref, ktf_ref, vf_ref,
                       kta_ref, ktb_ref, va_ref, vb_ref, nb_ref,
                       o_ref,
                       sc_ref, thr_ref, cut_ref, qs_ref, m_ref, l_ref, acc_ref,
                       *, k_top, n_kv, group, dh, idx_dim, n_pos_bits):
    s = pl.program_id(0)
    qb = qb_ref[s]
    kb = kb_ref[s]
    q0 = qb * QB
    coef = (dh ** -0.5) * LOG2E
    row = lax.broadcasted_iota(I32, (QB, 1), 0)
    limit = q0 + (row // CHUNK + 1) * CHUNK

    @pl.when(first_ref[s] == 1)
    def _setup():
        iq = iq_ref[...]
        qi = jnp.concatenate(
            [iq[:, h * idx_dim:(h + 1) * idx_dim] for h in range(N_IDX_HEADS)], axis=0).astype(BF16)
        w = ikw_ref[:, idx_dim:idx_dim + N_IDX_HEADS] * ((N_IDX_HEADS ** -0.5) * (idx_dim ** -0.5))
        n_sw = qb // SUB + 1

        def score_body(c, carry):
            ikc = jnp.concatenate([ikt_ref[c * SUB + j] for j in range(SUB)], axis=1)
            sc = _index_scores(qi, w, ikc, N_IDX_HEADS, QB)
            pos = c * KC + lax.broadcasted_iota(I32, (QB, KC), 1)
            key = _sort_key(jnp.where(pos < limit, sc, -jnp.inf))
            for j in range(SUB):
                sc_ref[c * SUB + j] = key[:, j * QB:(j + 1) * QB]
            return carry

        lax.fori_loop(0, n_sw, score_body, 0)

        def count_ge(cand):
            def body(c, acc):
                for j in range(SUB):
                    acc = acc + (sc_ref[c * SUB + j] >= cand).astype(I32)
                return acc
            acc = lax.fori_loop(0, n_sw, body, jnp.zeros((QB, QB), I32))
            return jnp.sum(acc, axis=1, keepdims=True)

        thr_k, cnt_k = _kth_largest_key(count_ge, k_top, QB, n_sw * KC)
        thr_ref[...] = thr_k
        cut_ref[...] = jnp.full((QB, 1), INT_MAX, I32)
        excess = cnt_k > k_top

        @pl.when(jnp.max(excess.astype(I32)) > 0)
        def _ties():
            def count_tie_below(p):
                def body(c, acc):
                    for j in range(SUB):
                        pos = (c * SUB + j) * QB + lax.broadcasted_iota(I32, (QB, QB), 1)
                        acc = acc + ((sc_ref[c * SUB + j] == thr_k) & (pos < p)).astype(I32)
                    return acc
                acc = lax.fori_loop(0, n_sw, body, jnp.zeros((QB, QB), I32))
                return jnp.sum(acc, axis=1, keepdims=True)

            need = k_top - count_ge(thr_k + 1)
            cut = _tie_cut(count_tie_below, need, n_pos_bits, QB)
            cut_ref[...] = jnp.where(excess, cut, INT_MAX)

        q = q_ref[...] * coef
        for g in range(n_kv):
            qs_ref[g] = jnp.concatenate(
                [q[:, (g * group + r) * dh:(g * group + r + 1) * dh] for r in range(group)],
                axis=0).astype(BF16)
        m_ref[...] = jnp.full(m_ref.shape, NEG, F32)
        l_ref[...] = jnp.zeros(l_ref.shape, F32)
        acc_ref[...] = jnp.zeros(acc_ref.shape, F32)

    thr = thr_ref[...]
    cut = cut_ref[...]

    @pl.when(kind_ref[s] == 0)
    def _far():
        pos = kb * FKC + lax.broadcasted_iota(I32, (QB, FKC), 1)
        key = jnp.concatenate([sc_ref[jnp.minimum(kb * FSUB + j, qb)] for j in range(FSUB)], axis=1)
        sel = _select(key, pos, thr, cut)
        sel = sel & (pos < q0 - QB)
        madd = jnp.where(sel, 0.0, NEG)
        lgs = [jnp.dot(qs_ref[g], ktf_ref[g * dh:(g + 1) * dh, :], preferred_element_type=F32)
               for g in range(n_kv)]
        for g in range(n_kv):
            _softmax_step(lgs[g], madd, lambda hh, q_off: None,
                          vf_ref[:, g * 2 * dh:(g + 1) * 2 * dh], m_ref, l_ref, acc_ref, g)

    @pl.when(kind_ref[s] == 1)
    def _window():
        w_keys = 2 * QB
        pos = q0 - QB + lax.broadcasted_iota(I32, (QB, w_keys), 1)
        key = jnp.concatenate([sc_ref[jnp.maximum(qb - 1, 0)], sc_ref[qb]], axis=1)
        sel = _select(key, pos, thr, cut) & (pos >= 0) & (pos < limit)
        kt = jnp.concatenate([kta_ref[...], ktb_ref[...]], axis=1)
        v = jnp.concatenate([va_ref[...], vb_ref[...]], axis=0)
        madd = jnp.where(sel, 0.0, NEG)
        outs = []
        for g in range(n_kv):
            lg = jnp.dot(qs_ref[g], kt[g * dh:(g + 1) * dh, :], preferred_element_type=F32)
            _softmax_step(lg, madd,
                          lambda hh, q_off, g=g: nb_ref[g * group + hh, q_off:q_off + SLAB, :],
                          v[:, g * 2 * dh:(g + 1) * 2 * dh], m_ref, l_ref, acc_ref, g)
            out_g = acc_ref[g] / l_ref[g]
            outs += [out_g[r * QB:(r + 1) * QB] for r in range(group)]
        o_ref[...] = jnp.concatenate(outs, axis=1).astype(o_ref.dtype)


def _t5_bucket(rel, n_buckets):
    nb = n_buckets // 2
    max_exact = nb // 2
    n = jnp.abs(rel)
    nf = jnp.maximum(n, 1).astype(F32)
    large = max_exact + (jnp.log(nf / max_exact) / math.log(MAX_DISTANCE / max_exact)
                         * (nb - max_exact)).astype(I32)
    large = jnp.minimum(large, nb - 1)
    return jnp.where(rel > 0, nb, 0) + jnp.where(n < max_exact, n, large)


def _bias_lookup(rel_bias, bucket):
    out = jnp.zeros(bucket.shape + (rel_bias.shape[1],), F32)
    for b in range(rel_bias.shape[0]):
        out = jnp.where((bucket == b)[..., None], rel_bias[b].astype(F32), out)
    return out


def _dsa_prompt(proj, lay, ik_t3, k_t, v_bf, rel_bias, t, n_kv, dh, d_attn, idx_dim):
    n_heads = d_attn // dh
    group = n_heads // n_kv
    d_kv = n_kv * dh
    nqb = t // QB
    k_top = min(TOPK_MAX, t // 4)
    n_buckets = rel_bias.shape[0]
    assert dh == LANES, "row statistics are kept replicated across one lane tile"
    v_bf = jnp.concatenate([v_bf.reshape(t, n_kv, dh), jnp.ones((t, n_kv, dh), v_bf.dtype)],
                           axis=-1).reshape(t, 2 * d_kv)

    rel = (jnp.arange(2 * QB, dtype=I32)[None, :] - QB) - jnp.arange(QB, dtype=I32)[:, None]
    nb = _bias_lookup(rel_bias, _t5_bucket(rel, n_buckets))
    nb = jnp.moveaxis(nb, -1, 0) - rel_bias[n_buckets // 2 - 1].astype(F32)[:, None, None]
    nb = nb * LOG2E

    s_qb, s_kb, s_kind, s_first = [], [], [], []
    for qb in range(nqb):
        n_far = -(-max(qb - 1, 0) * QB // FKC)
        for kb in range(n_far):
            s_qb.append(qb); s_kb.append(kb); s_kind.append(0); s_first.append(int(kb == 0))
        s_qb.append(qb); s_kb.append(max(n_far - 1, 0)); s_kind.append(1); s_first.append(int(n_far == 0))
    tabs = [jnp.asarray(np.asarray(a, np.int32)) for a in (s_qb, s_kb, s_kind, s_first)]
    n_steps = len(s_qb)

    cq = lay["q"] // d_attn
    ciq = lay["iq"] // (N_IDX_HEADS * idx_dim)
    cikw = lay["ikw"] // LANES
    prev = lambda s, qb, kb, kd, fs: jnp.maximum(qb[s] - 1, 0)

    grid_spec = pltpu.PrefetchScalarGridSpec(
        num_scalar_prefetch=4,
        grid=(n_steps,),
        in_specs=[
            pl.BlockSpec((QB, d_attn), lambda s, qb, kb, kd, fs: (qb[s], cq)),
            pl.BlockSpec((QB, N_IDX_HEADS * idx_dim), lambda s, qb, kb, kd, fs: (qb[s], ciq)),
            pl.BlockSpec((QB, LANES), lambda s, qb, kb, kd, fs: (qb[s], cikw)),
            pl.BlockSpec((nqb, idx_dim, QB), lambda s, qb, kb, kd, fs: (0, 0, 0)),
            pl.BlockSpec((d_kv, FKC), lambda s, qb, kb, kd, fs: (0, kb[s])),
            pl.BlockSpec((FKC, 2 * d_kv), lambda s, qb, kb, kd, fs: (kb[s], 0)),
            pl.BlockSpec((d_kv, QB), lambda s, qb, kb, kd, fs: (0, prev(s, qb, kb, kd, fs))),
            pl.BlockSpec((d_kv, QB), lambda s, qb, kb, kd, fs: (0, qb[s])),
            pl.BlockSpec((QB, 2 * d_kv), lambda s, qb, kb, kd, fs: (prev(s, qb, kb, kd, fs), 0)),
            pl.BlockSpec((QB, 2 * d_kv), lambda s, qb, kb, kd, fs: (qb[s], 0)),
            pl.BlockSpec((n_heads, QB, 2 * QB), lambda s, qb, kb, kd, fs: (0, 0, 0)),
        ],
        out_specs=pl.BlockSpec((QB, d_attn), lambda s, qb, kb, kd, fs: (qb[s], 0)),
        scratch_shapes=[
            pltpu.VMEM((nqb, QB, QB), I32),
            pltpu.VMEM((QB, 1), I32),
            pltpu.VMEM((QB, 1), I32),
            pltpu.VMEM((n_kv, group * QB, dh), BF16),
            pltpu.VMEM((n_kv, group * QB, LANES), F32),
            pltpu.VMEM((n_kv, group * QB, LANES), F32),
            pltpu.VMEM((n_kv, group * QB, dh), F32),
        ],
    )
    kern = functools.partial(_dsa_prompt_kernel, k_top=k_top, n_kv=n_kv, group=group, dh=dh,
                             idx_dim=idx_dim, n_pos_bits=(t - 1).bit_length())
    return pl.pallas_call(
        kern,
        out_shape=jax.ShapeDtypeStruct((t, d_attn), BF16),
        grid_spec=grid_spec,
        compiler_params=_cparams(1, 48),
        name="dsa_prompt",
    )(*tabs, proj, proj, proj, ik_t3, k_t, v_bf, k_t, k_t, v_bf, v_bf, nb)


def _dsa_sample_kernel(q_ref, iq_ref, ikw_ref, cik_ref, ck_ref, cv_ref, nik_ref, nk_ref, nv_ref,
                       bias_ref, o_ref, *, k_top, n_kv, group, dh, idx_dim, past, n_keys):
    s_len = q_ref.shape[0]
    lp = past + nk_ref.shape[0]
    scale = dh ** -0.5
    ik_all = jnp.concatenate([cik_ref[...].astype(BF16), nik_ref[...]], axis=0)
    k_all = jnp.concatenate([ck_ref[...].astype(BF16), nk_ref[...]], axis=0)
    v_all = jnp.concatenate([cv_ref[...].astype(BF16), nv_ref[...]], axis=0)
    iq = iq_ref[...]
    qi = jnp.concatenate(
        [iq[:, h * idx_dim:(h + 1) * idx_dim] for h in range(N_IDX_HEADS)], axis=0).astype(BF16)
    w = ikw_ref[:, idx_dim:idx_dim + N_IDX_HEADS] * ((N_IDX_HEADS ** -0.5) * (idx_dim ** -0.5))
    sc = _index_scores(qi, w, ik_all, N_IDX_HEADS, s_len, keys_major=True)
    row = lax.broadcasted_iota(I32, (s_len, 1), 0)
    limit = jnp.minimum(((past + row) // CHUNK + 1) * CHUNK, n_keys)
    pos = lax.broadcasted_iota(I32, (s_len, lp), 1)
    allowed = pos < limit
    key = _sort_key(jnp.where(allowed, sc, -jnp.inf))

    def count_ge(cand):
        return jnp.sum((key >= cand).astype(I32), axis=1, keepdims=True)

    thr, _ = _kth_largest_key(count_ge, k_top, s_len, lp)

    def count_tie_below(p):
        return jnp.sum(((key == thr) & (pos < p)).astype(I32), axis=1, keepdims=True)

    cut = _tie_cut(count_tie_below, k_top - count_ge(thr + 1), (lp - 1).bit_length(), s_len)
    sel = _select(key, pos, thr, cut) & allowed
    q = q_ref[...]
    outs = []
    for g in range(n_kv):
        qg = jnp.concatenate(
            [q[:, (g * group + r) * dh:(g * group + r + 1) * dh] for r in range(group)],
            axis=0).astype(BF16)
        lg = lax.dot_general(qg, k_all[:, g * dh:(g + 1) * dh], _NT, preferred_element_type=F32) * scale
        lg = lg.reshape(group, s_len, lp) + bias_ref[g * group:(g + 1) * group]
        lm = jnp.where(sel[None], lg, NEG).reshape(group * s_len, lp)
        p = jnp.exp(lm - jnp.max(lm, axis=1, keepdims=True))
        den = jnp.sum(p, axis=1, keepdims=True)
        out_g = jnp.dot(p.astype(BF16), v_all[:, g * dh:(g + 1) * dh], preferred_element_type=F32) / den
        outs += [out_g[r * s_len:(r + 1) * s_len] for r in range(group)]
    o_ref[...] = jnp.concatenate(outs, axis=1).astype(o_ref.dtype)


def _dsa_sample(proj, lay, caches, news, rel_bias, row0, n_seq, s_len, past, n_kv, dh, d_attn,
                idx_dim):
    n_heads = d_attn // dh
    group = n_heads // n_kv
    d_kv = n_kv * dh
    lp = past + LANES
    n_keys = past + s_len
    k_top = min(TOPK_MAX, n_keys // 4)
    rel = jnp.arange(lp, dtype=I32)[None, :] - (past + jnp.arange(s_len, dtype=I32))[:, None]
    bias = jnp.moveaxis(_bias_lookup(rel_bias, _t5_bucket(rel, rel_bias.shape[0])), -1, 0)
    rb = row0 // s_len
    cq = lay["q"] // d_attn
    ciq = lay["iq"] // (N_IDX_HEADS * idx_dim)
    cikw = lay["ikw"] // LANES
    kern = functools.partial(_dsa_sample_kernel, k_top=k_top, n_kv=n_kv, group=group, dh=dh,
                             idx_dim=idx_dim, past=past, n_keys=n_keys)
    return pl.pallas_call(
        kern,
        out_shape=jax.ShapeDtypeStruct((n_seq * s_len, d_attn), BF16),
        grid=(n_seq,),
        in_specs=[
            pl.BlockSpec((s_len, d_attn), lambda b: (rb + b, cq)),
            pl.BlockSpec((s_len, N_IDX_HEADS * idx_dim), lambda b: (rb + b, ciq)),
            pl.BlockSpec((s_len, LANES), lambda b: (rb + b, cikw)),
            pl.BlockSpec((None, past, idx_dim), lambda b: (b, 0, 0)),
            pl.BlockSpec((None, past, d_kv), lambda b: (b, 0, 0)),
            pl.BlockSpec((None, past, d_kv), lambda b: (b, 0, 0)),
            pl.BlockSpec((None, LANES, idx_dim), lambda b: (b, 0, 0)),
            pl.BlockSpec((None, LANES, d_kv), lambda b: (b, 0, 0)),
            pl.BlockSpec((None, LANES, d_kv), lambda b: (b, 0, 0)),
            pl.BlockSpec((n_heads, s_len, lp), lambda b: (0, 0, 0)),
        ],
        out_specs=pl.BlockSpec((s_len, d_attn), lambda b: (b, 0)),
        compiler_params=_cparams(1, 32),
        name="dsa_sample",
    )(proj, proj, proj, *caches, *news, bias)


def _merge_kernel(yc_ref, ya_ref, gc_ref, ga_ref, wc_ref, wa_ref, o_ref, wcb_ref, wab_ref):
    @pl.when(pl.program_id(1) == 0)
    def _():
        wcb_ref[...] = wc_ref[...].astype(BF16)
        wab_ref[...] = wa_ref[...].astype(BF16)

    bc = jnp.dot(yc_ref[...], wcb_ref[...], preferred_element_type=F32)
    ba = jnp.dot(ya_ref[...], wab_ref[...], preferred_element_type=F32)
    merged = jax.nn.sigmoid(gc_ref[...]) * bc + jax.nn.sigmoid(ga_ref[...]) * ba
    o_ref[...] = merged.astype(o_ref.dtype)


def _merge(y_conv, y_attn, gates_proj, w_bc, w_ba):
    m, dc = y_conv.shape
    da = y_attn.shape[1]
    d = w_bc.shape[1]
    tm = _pick_tile(m, 640, 16)
    tn = _pick_tile(d, 512, 128)
    cgc, cga = 0, d // tn
    return pl.pallas_call(
        _merge_kernel,
        out_shape=jax.ShapeDtypeStruct((m, d), BF16),
        grid=(d // tn, m // tm),
        in_specs=[pl.BlockSpec((tm, dc), lambda j, i: (i, 0)),
                  pl.BlockSpec((tm, da), lambda j, i: (i, 0)),
                  pl.BlockSpec((tm, tn), lambda j, i: (i, cgc + j)),
                  pl.BlockSpec((tm, tn), lambda j, i: (i, cga + j)),
                  pl.BlockSpec((dc, tn), lambda j, i: (0, j)),
                  pl.BlockSpec((da, tn), lambda j, i: (0, j))],
        out_specs=pl.BlockSpec((tm, tn), lambda j, i: (i, j)),
        scratch_shapes=[pltpu.VMEM((dc, tn), BF16), pltpu.VMEM((da, tn), BF16)],
        compiler_params=_cparams(2, 48),
        name="merge",
    )(y_conv, y_attn, gates_proj, gates_proj, w_bc, w_ba)


def _out_kernel(a_ref, x_ref, w_ref, o_ref, wb_ref):
    @pl.when(pl.program_id(1) == 0)
    def _():
        wb_ref[...] = w_ref[...].astype(BF16)

    o_ref[...] = x_ref[...] + jnp.dot(a_ref[...], wb_ref[...], preferred_element_type=F32)


def _out_proj(merged, x, w_out):
    m, k = merged.shape
    n = w_out.shape[1]
    tm = _pick_tile(m, 640, 16)
    tn = _pick_tile(n, 512, 128)
    return pl.pallas_call(
        _out_kernel,
        out_shape=jax.ShapeDtypeStruct((m, n), F32),
        grid=(n // tn, m // tm),
        in_specs=[pl.BlockSpec((tm, k), lambda j, i: (i, 0)),
                  pl.BlockSpec((tm, tn), lambda j, i: (i, j)),
                  pl.BlockSpec((k, tn), lambda j, i: (0, j))],
        out_specs=pl.BlockSpec((tm, tn), lambda j, i: (i, j)),
        scratch_shapes=[pltpu.VMEM((k, tn), BF16)],
        compiler_params=_cparams(2, 48),
        name="out_proj",
    )(merged, x, w_out)


def _split_bf16(x):
    hi = x.astype(BF16)
    lo = (x - hi.astype(F32)).astype(BF16)
    return hi, lo


def _router_kernel(x_ref, g_ref, rw_ref, rb_ref, h_ref, idx_ref, gate_ref, rank_ref, cnt_ref, carry_ref):
    @pl.when(pl.program_id(0) == 0)
    def _():
        carry_ref[...] = jnp.zeros(carry_ref.shape, F32)

    xf = x_ref[...]
    xf = xf * lax.rsqrt(jnp.mean(xf * xf, axis=-1, keepdims=True) + EPS)
    h = xf * g_ref[...]
    h_ref[...] = h
    h_hi, h_lo = _split_bf16(h)
    w_hi, w_lo = _split_bf16(rw_ref[...])
    logits = (jnp.dot(h_hi, w_hi, preferred_element_type=F32)
              + (jnp.dot(h_hi, w_lo, preferred_element_type=F32)
                 + jnp.dot(h_lo, w_hi, preferred_element_type=F32)))
    logits = logits + rb_ref[...]
    n_e = logits.shape[1]
    lane = lax.broadcasted_iota(I32, logits.shape, 1)
    vals, idxs = [], []
    cur = logits
    for _ in range(EXPERT_TOP_K):
        mx = jnp.max(cur, axis=1, keepdims=True)
        ix = jnp.min(jnp.where(cur == mx, lane, n_e), axis=1, keepdims=True)
        vals.append(mx)
        idxs.append(ix)
        cur = jnp.where(lane == ix, -jnp.inf, cur)
    top = jnp.concatenate(vals, axis=1)
    e = jnp.exp(top - vals[0])
    gate_ref[...] = e / jnp.sum(e, axis=1, keepdims=True)
    idx_ref[...] = jnp.concatenate(idxs, axis=1)

    tm = logits.shape[0]
    hot = [(lane == ix).astype(F32) for ix in idxs]
    total = hot[0]
    for oh in hot[1:]:
        total = total + oh
    ri = lax.broadcasted_iota(I32, (tm, tm), 0)
    ci = lax.broadcasted_iota(I32, (tm, tm), 1)
    tri = (ci < ri).astype(BF16)
    before = jnp.dot(tri, total.astype(BF16), preferred_element_type=F32) + carry_ref[...]
    rank_ref[...] = jnp.concatenate(
        [jnp.sum(oh * before, axis=1, keepdims=True) for oh in hot], axis=1).astype(I32)
    carry_ref[...] = carry_ref[...] + jnp.sum(total, axis=0, keepdims=True)
    cnt_ref[...] = carry_ref[...].astype(I32)


def _router(x1, g_ffn, router_w, router_b):
    m, d = x1.shape
    n_e = router_w.shape[1]
    tm = _pick_tile(m, 256, SUBLANES)
    row_spec = pl.BlockSpec((tm, EXPERT_TOP_K), lambda i: (i, 0))
    return pl.pallas_call(
        _router_kernel,
        out_shape=(jax.ShapeDtypeStruct((m, d), F32),
                   jax.ShapeDtypeStruct((m, EXPERT_TOP_K), I32),
                   jax.ShapeDtypeStruct((m, EXPERT_TOP_K), F32),
                   jax.ShapeDtypeStruct((m, EXPERT_TOP_K), I32),
                   jax.ShapeDtypeStruct((1, n_e), I32)),
        grid=(m // tm,),
        in_specs=[pl.BlockSpec((tm, d), lambda i: (i, 0)),
                  pl.BlockSpec((1, d), lambda i: (0, 0)),
                  pl.BlockSpec((d, n_e), lambda i: (0, 0)),
                  pl.BlockSpec((1, n_e), lambda i: (0, 0))],
        out_specs=(pl.BlockSpec((tm, d), lambda i: (i, 0)), row_spec, row_spec, row_spec,
                   pl.BlockSpec((1, n_e), lambda i: (0, 0))),
        scratch_shapes=[pltpu.VMEM((1, n_e), F32)],
        compiler_params=_cparams(1, 48),
        name="router",
    )(x1, g_ffn.reshape(1, d), router_w, router_b.reshape(1, n_e))


def _row_copy(src_hbm, buf, sem, src_row, dst_row):
    return pltpu.make_async_copy(src_hbm.at[src_row], buf.at[dst_row], sem)


RUN_BLOCKS = 5
GATHER_ROWS = 128


def _up_kernel(re_ref, rc_ref,
               tok_hbm, h_hbm, wg_ref, wl_ref, bg_ref, bl_ref, o_ref,
               xs_ref, stage_ref, tok_smem, wgb_ref, wlb_ref, sem_idx, sem):
    j = pl.program_id(0)
    cnt = rc_ref[j]
    tm = xs_ref.shape[1]
    n_burst = tm // GATHER_ROWS

    def gather_block(r, carry):
        cp = pltpu.make_async_copy(tok_hbm.at[j * RUN_BLOCKS + r], tok_smem, sem_idx)
        cp.start()
        cp.wait()

        def issue(c):
            def body(i, c2):
                _row_copy(h_hbm, stage_ref.at[c % 2], sem.at[c % 2],
                          tok_smem[c * GATHER_ROWS + i], i).start()
                return c2
            lax.fori_loop(0, GATHER_ROWS, body, 0, unroll=8)

        def drain(c):
            def body(i, c2):
                _row_copy(h_hbm, stage_ref.at[c % 2], sem.at[c % 2], 0, i).wait()
                return c2
            lax.fori_loop(0, GATHER_ROWS, body, 0, unroll=8)

        issue(0)
        for c in range(n_burst):
            if c + 1 < n_burst:
                issue(c + 1)
            drain(c)
            xs_ref[r, c * GATHER_ROWS:(c + 1) * GATHER_ROWS, :] = stage_ref[c % 2].astype(BF16)
        return carry

    @pl.when(pl.program_id(1) == 0)
    def _gather():
        lax.fori_loop(0, cnt, gather_block, 0)

    @pl.when(cnt > 0)
    def _cast():
        wgb_ref[...] = wg_ref[...].astype(BF16)
        wlb_ref[...] = wl_ref[...].astype(BF16)

    def compute_block(r, carry):
        x = xs_ref[r]
        hg = jnp.dot(x, wgb_ref[...], preferred_element_type=F32) + bg_ref[...]
        hl = jnp.dot(x, wlb_ref[...], preferred_element_type=F32) + bl_ref[...]
        x_glu = jnp.minimum(hg, SWIGLU_LIMIT)
        x_lin = jnp.clip(hl, -SWIGLU_LIMIT, SWIGLU_LIMIT)
        act = x_glu * jax.nn.sigmoid(SWIGLU_ALPHA * x_glu) * (x_lin + 1.0)
        o_ref[pl.ds(pl.multiple_of(r * tm, tm), tm), :] = act.astype(o_ref.dtype)
        return carry

    def zero_block(r, carry):
        o_ref[pl.ds(pl.multiple_of(r * tm, tm), tm), :] = jnp.zeros((tm, o_ref.shape[1]), o_ref.dtype)
        return carry

    lax.fori_loop(0, cnt, compute_block, 0)
    lax.fori_loop(cnt, RUN_BLOCKS, zero_block, 0)


def _expert_up(h2, row_tok, run_e, run_cnt, w_up, b_up, tm, tf):
    d = h2.shape[1]
    n_e, _, f2 = w_up.shape
    f = f2 // 2
    nf = f // tf
    n_rows = row_tok.shape[0]
    n_runs = run_e.shape[0]
    wtile = lambda half: (lambda j, t, re, rc: (re[j], 0, half * nf + jnp.where(rc[j] > 0, t, nf - 1)))
    grid_spec = pltpu.PrefetchScalarGridSpec(
        num_scalar_prefetch=2,
        grid=(n_runs, nf),
        in_specs=[
            pl.BlockSpec(memory_space=pl.ANY),
            pl.BlockSpec(memory_space=pl.ANY),
            pl.BlockSpec((None, d, tf), wtile(0)),
            pl.BlockSpec((None, d, tf), wtile(1)),
            pl.BlockSpec((None, 1, tf), wtile(0)),
            pl.BlockSpec((None, 1, tf), wtile(1)),
        ],
        out_specs=pl.BlockSpec((RUN_BLOCKS * tm, tf), lambda j, t, re, rc: (j, t)),
        scratch_shapes=[pltpu.VMEM((RUN_BLOCKS, tm, d), BF16),
                        pltpu.VMEM((2, GATHER_ROWS, d), h2.dtype),
                        pltpu.SMEM((tm,), I32),
                        pltpu.VMEM((d, tf), BF16), pltpu.VMEM((d, tf), BF16),
                        pltpu.SemaphoreType.DMA(()),
                        pltpu.SemaphoreType.DMA((2,))],
    )
    return pl.pallas_call(
        _up_kernel,
        out_shape=jax.ShapeDtypeStruct((n_rows, f), BF16),
        grid_spec=grid_spec,
        compiler_params=_cparams(2, 56),
        name="moe_up",
    )(run_e, run_cnt, row_tok.reshape(n_rows // tm, tm), h2, w_up, w_up,
      b_up.reshape(n_e, 1, f2), b_up.reshape(n_e, 1, f2))


def _down_kernel(re_ref, rc_ref, a_hbm, w_ref, b_ref, o_ref, as_ref, wb_ref, sem):
    j = pl.program_id(0)
    cnt = rc_ref[j]
    tm = as_ref.shape[1]

    def block_copy(r):
        row0 = pl.multiple_of((j * RUN_BLOCKS + r) * tm, tm)
        return pltpu.make_async_copy(a_hbm.at[pl.ds(row0, tm)], as_ref.at[r], sem)

    @pl.when(pl.program_id(1) == 0)
    def _load():
        def start(r, c):
            block_copy(r).start()
            return c

        def wait(r, c):
            block_copy(r).wait()
            return c

        lax.fori_loop(0, cnt, start, 0)
        lax.fori_loop(0, cnt, wait, 0)

    @pl.when(cnt > 0)
    def _cast():
        wb_ref[...] = w_ref[...].astype(BF16)

    def compute_block(r, carry):
        y = jnp.dot(as_ref[r], wb_ref[...], preferred_element_type=F32) + b_ref[...]
        o_ref[pl.ds(pl.multiple_of(r * tm, tm), tm), :] = y
        return carry

    def zero_block(r, carry):
        o_ref[pl.ds(pl.multiple_of(r * tm, tm), tm), :] = jnp.zeros((tm, o_ref.shape[1]), o_ref.dtype)
        return carry

    lax.fori_loop(0, cnt, compute_block, 0)
    lax.fori_loop(cnt, RUN_BLOCKS, zero_block, 0)


def _expert_down(act, run_e, run_cnt, w_down, b_down, tm, tn):
    n_rows, f = act.shape
    n_e, _, d = w_down.shape
    nt = d // tn
    n_runs = run_e.shape[0]
    wtile = lambda j, t, re, rc: (re[j], 0, jnp.where(rc[j] > 0, t, nt - 1))
    grid_spec = pltpu.PrefetchScalarGridSpec(
        num_scalar_prefetch=2,
        grid=(n_runs, nt),
        in_specs=[
            pl.BlockSpec(memory_space=pl.ANY),
            pl.BlockSpec((None, f, tn), wtile),
            pl.BlockSpec((None, 1, tn), wtile),
        ],
        out_specs=pl.BlockSpec((RUN_BLOCKS * tm, tn), lambda j, t, re, rc: (j, t)),
        scratch_shapes=[pltpu.VMEM((RUN_BLOCKS, tm, f), BF16),
                        pltpu.VMEM((f, tn), BF16),
                        pltpu.SemaphoreType.DMA(())],
    )
    return pl.pallas_call(
        _down_kernel,
        out_shape=jax.ShapeDtypeStruct((n_rows, d), F32),
        grid_spec=grid_spec,
        compiler_params=_cparams(2, 56),
        name="moe_down",
    )(run_e, run_cnt, act, w_down, b_down.reshape(n_e, 1, d))


def _combine_kernel(pos_hbm, y_hbm, x_ref, gt_ref, g_ref, o_ref, pos_smem, buf, sem_idx, sem):
    i = pl.program_id(0)
    n_steps = pl.num_programs(0)
    tb = x_ref.shape[0]
    n = pos_smem.shape[1]
    cur = i % 2
    nxt = 1 - cur

    def idx_copy(step, sl):
        return pltpu.make_async_copy(pos_hbm.at[step], pos_smem.at[sl], sem_idx.at[sl])

    def issue_rows(sl):
        def body(h, c):
            for prio in range(2):
                r = 2 * h + prio
                _row_copy(y_hbm, buf.at[sl], sem.at[sl], pos_smem[sl, r], r).start(priority=prio)
            return c
        lax.fori_loop(0, n // 2, body, 0, unroll=4)

    @pl.when(i == 0)
    def _():
        idx_copy(0, 0).start()
        idx_copy(0, 0).wait()
        issue_rows(0)

        @pl.when(n_steps > 1)
        def _():
            idx_copy(1, 1).start()

    @pl.when(i + 1 < n_steps)
    def _():
        idx_copy(i + 1, nxt).wait()
        issue_rows(nxt)

    @pl.when(i + 2 < n_steps)
    def _():
        idx_copy(i + 2, cur).start()

    def drain(r, c):
        _row_copy(y_hbm, buf.at[cur], sem.at[cur], 0, r).wait()
        return c

    lax.fori_loop(0, n, drain, 0, unroll=8)
    gt = gt_ref[...]
    moe = None
    for j in range(EXPERT_TOP_K):
        term = buf[cur, j * tb:(j + 1) * tb, :] * gt[:, j:j + 1]
        moe = term if moe is None else moe + term
    xf = x_ref[...] + moe
    xf = xf * lax.rsqrt(jnp.mean(xf * xf, axis=-1, keepdims=True) + EPS)
    o_ref[...] = xf * g_ref[...]


def _combine(y_rows, pos, gates, x1, g_final, tb):
    m, d = x1.shape
    nb = m // tb
    n = EXPERT_TOP_K * tb
    pos_blocks = pos.reshape(nb, tb, EXPERT_TOP_K).transpose(0, 2, 1).reshape(nb, n)
    return pl.pallas_call(
        _combine_kernel,
        out_shape=jax.ShapeDtypeStruct((m, d), F32),
        grid=(nb,),
        in_specs=[pl.BlockSpec(memory_space=pl.ANY), pl.BlockSpec(memory_space=pl.ANY),
                  pl.BlockSpec((tb, d), lambda i: (i, 0)),
                  pl.BlockSpec((tb, EXPERT_TOP_K), lambda i: (i, 0)),
                  pl.BlockSpec((1, d), lambda i: (0, 0))],
        out_specs=pl.BlockSpec((tb, d), lambda i: (i, 0)),
        scratch_shapes=[pltpu.SMEM((2, n), I32),
                        pltpu.VMEM((2, n, d), F32),
                        pltpu.SemaphoreType.DMA((2,)),
                        pltpu.SemaphoreType.DMA((2,))],
        compiler_params=_cparams(1, 32),
        name="moe_combine",
    )(pos_blocks, y_rows, x1, gates, g_final.reshape(1, d))


def _routing_tables(top_idx, rank, counts, tm):
    t = top_idx.shape[0]
    n = t * EXPERT_TOP_K
    n_experts = counts.shape[0]
    n_runs = (-(-n // tm) + n_experts) // RUN_BLOCKS + n_experts
    nblk_e = (counts + tm - 1) // tm
    nrun_e = (nblk_e + RUN_BLOCKS - 1) // RUN_BLOCKS
    run_end = jnp.cumsum(nrun_e)
    run_start = run_end - nrun_e
    experts = jnp.arange(n_experts, dtype=I32)
    row0 = run_start * (RUN_BLOCKS * tm)
    hot = top_idx[:, :, None] == experts[None, None, :]
    pos = jnp.sum(jnp.where(hot, row0[None, None, :], 0), axis=-1) + rank
    tok = jnp.broadcast_to(jnp.arange(t, dtype=I32)[:, None], pos.shape)
    row_tok = jnp.zeros((n_runs * RUN_BLOCKS * tm,), I32).at[pos.reshape(n)].set(tok.reshape(n))
    j = jnp.arange(n_runs, dtype=I32)
    e_j = jnp.minimum(jnp.sum((run_end[None, :] <= j[:, None]).astype(I32), axis=1), n_experts - 1)
    sel = e_j[:, None] == experts[None, :]
    pick = lambda a: jnp.sum(jnp.where(sel, a[None, :], 0), axis=1)
    in_e = j - pick(run_start)
    run_cnt = jnp.clip(pick(nblk_e) - in_e * RUN_BLOCKS, 0, RUN_BLOCKS)
    run_cnt = jnp.where(j < run_end[-1], run_cnt, 0).astype(I32)
    last_e = jnp.sum(jnp.where(j == run_end[-1] - 1, e_j, 0))
    run_e = jnp.where(run_cnt > 0, e_j, last_e).astype(I32)
    return pos, row_tok, run_e, run_cnt


def _proj_layout(d, dc, d_attn, d_kv, idx_dim):
    iq_w = N_IDX_HEADS * idx_dim
    assert idx_dim + N_IDX_HEADS <= LANES
    lay, off = {}, 0
    for name, width in (("u", dc), ("c", dc), ("b", dc), ("q", d_attn), ("k", d_kv), ("v", d_kv),
                        ("iq", iq_w), ("ikw", idx_dim + N_IDX_HEADS)):
        lay[name] = off
        off += width
    for name, width in (("u", dc), ("c", dc), ("b", dc), ("q", d_attn), ("iq", iq_w), ("ikw", LANES)):
        assert lay[name] % width == 0, (name, lay[name], width)
    lay["gate_src"] = off
    lay["main_cols"] = -(-(lay["ikw"] + LANES) // 512) * 512
    assert lay["main_cols"] <= off + 2 * d
    return lay


def kernel(x_prompt, x_sample, cache_k, cache_v, cache_idx_k, state_conv, g_mix, w_in, conv_w,
           w_branch_conv, w_branch_attn, w_out, rel_bias, g_ffn, router_w, router_b, w_up, b_up,
           w_down, b_down, g_final):
    depth = w_in.shape[0]
    assert depth == 1, "one trunk layer"
    bp, t, d = x_prompt.shape
    assert bp == 1
    n_seq, s_len, _ = x_sample.shape
    past, n_kv, dh = cache_k.shape[2:5]
    idx_dim = cache_idx_k.shape[-1]
    dc = conv_w.shape[-1]
    d_attn = w_branch_attn.shape[1]
    d_kv = n_kv * dh
    n_experts = router_w.shape[-1]
    assert t % FKC == 0 and s_len % 16 == 0 and s_len <= LANES and past % 16 == 0

    lay = _proj_layout(d, dc, d_attn, d_kv, idx_dim)
    assert lay["gate_src"] + 2 * d == w_in.shape[2]

    h, x_all = _rmsnorm_rows(x_prompt.reshape(t, d), x_sample.reshape(n_seq * s_len, d), g_mix[0])
    proj = _matmul(h, w_in[0], n_cols=lay["main_cols"])
    gates_proj = _matmul(h, w_in[0][:, lay["gate_src"]:])

    k_new = proj[:, lay["k"]:lay["k"] + d_kv]
    v_new = proj[:, lay["v"]:lay["v"] + d_kv]
    ik_new = proj[:, lay["ikw"]:lay["ikw"] + idx_dim]

    y_conv, st_p, st_s = _short_conv(proj, lay, conv_w[0], state_conv[0], t, n_seq, s_len)

    k_t = k_new[:t].T.astype(BF16)
    v_bf = v_new[:t].astype(BF16)
    ik_t3 = ik_new[:t].reshape(t // QB, QB, idx_dim).transpose(0, 2, 1).astype(BF16)
    ya_p = _dsa_prompt(proj, lay, ik_t3, k_t, v_bf, rel_bias, t, n_kv, dh, d_attn, idx_dim)

    def new_rows(new, width):
        rows = new[t:].reshape(n_seq, s_len, width).astype(BF16)
        return jnp.pad(rows, ((0, 0), (0, LANES - s_len), (0, 0)))

    caches = (cache_idx_k[0], cache_k[0].reshape(n_seq, past, d_kv), cache_v[0].reshape(n_seq, past, d_kv))
    news = (new_rows(ik_new, idx_dim), new_rows(k_new, d_kv), new_rows(v_new, d_kv))
    ya_s = _dsa_sample(proj, lay, caches, news, rel_bias,
                       t, n_seq, s_len, past, n_kv, dh, d_attn, idx_dim)
    y_attn = jnp.concatenate([ya_p, ya_s], axis=0)

    merged = _merge(y_conv, y_attn, gates_proj, w_branch_conv[0], w_branch_attn[0])
    x1 = _out_proj(merged, x_all, w_out[0])

    h2, top_idx, gates, rank, counts = _router(x1, g_ffn[0], router_w[0], router_b[0])
    tm_e = 512 if (x_all.shape[0] * EXPERT_TOP_K) >= 16 * 512 else 128
    pos, row_tok, run_e, run_cnt = _routing_tables(top_idx, rank, counts.reshape(n_experts), tm_e)
    tf = _pick_tile(w_down.shape[2], 256, LANES)
    tn = _pick_tile(d, 512, LANES)
    act = _expert_up(h2, row_tok, run_e, run_cnt, w_up[0], b_up[0], tm_e, tf)
    y_rows = _expert_down(act, run_e, run_cnt, w_down[0], b_down[0], tm_e, tn)
    tb = _pick_tile(x_all.shape[0], 64, SUBLANES)
    y_all = _combine(y_rows, pos, gates, x1, g_final, tb)

    y_prompt = y_all[:t].reshape(1, t, d)
    y_sample = y_all[t:].reshape(n_seq, s_len, d)
    return (y_prompt, y_sample,
            k_new[:t].reshape(1, 1, t, n_kv, dh), v_new[:t].reshape(1, 1, t, n_kv, dh),
            ik_new[:t].reshape(1, 1, t, idx_dim), st_p.reshape(1, 1, 2, dc),
            k_new[t:].reshape(1, n_seq, s_len, n_kv, dh), v_new[t:].reshape(1, n_seq, s_len, n_kv, dh),
            ik_new[t:].reshape(1, n_seq, s_len, idx_dim), st_s.reshape(1, n_seq, 2, dc))
```
